```python
import math
import jax, jax.numpy as jnp
from jax import lax
import numpy as np

D_MODEL = 1024
BATCH = 8
SEQ = 4096
DEPTH = 4
DEC_BATCH = 4
DEC_SEQ = 8192
PAST_LEN = 128

HEAD_DIM = 64
MIX_WIDTH = D_MODEL
N_HEADS_A = MIX_WIDTH // (2 * HEAD_DIM)
N_KV_A = N_HEADS_A // 4
N_HEADS_B = MIX_WIDTH // (2 * HEAD_DIM)
N_KV_B = N_HEADS_B // 4
Q_A = N_HEADS_A * HEAD_DIM
KV_A = N_KV_A * HEAD_DIM
Q_B = N_HEADS_B * HEAD_DIM
KV_B = N_KV_B * HEAD_DIM
D_IN = Q_A + 2 * KV_A + Q_B + 2 * KV_B
WINDOW = 128
BLOCK = 128
NUM_BUCKETS = 32
MAX_DISTANCE = 128
GRID_W = 64
ROPE_THETA = 10000.0
D_FF = ((8 * D_MODEL + 3 * 256 - 1) // (3 * 256)) * 256
DEEPNORM_ALPHA = (2.0 * DEPTH) ** 0.25
DEEPNORM_BETA = (8.0 * DEPTH) ** -0.25
LN_EPS = 1e-5
RMS_EPS = 1e-6
NEG_INF = -1e30

kernel_name = "hymba_style_window_axial_encoder"


def layer_norm(x, g, b):
    xf = x.astype(jnp.float32)
    mu = jnp.mean(xf, axis=-1, keepdims=True)
    var = jnp.mean(jnp.square(xf - mu), axis=-1, keepdims=True)
    y = (xf - mu) * lax.rsqrt(var + LN_EPS) * g.astype(jnp.float32) + b.astype(jnp.float32)
    return y.astype(x.dtype)


def head_rms_norm(x, g):
    xf = x.astype(jnp.float32)
    y = xf * lax.rsqrt(jnp.mean(jnp.square(xf), axis=-1, keepdims=True) + RMS_EPS)
    return y * g.astype(jnp.float32)


def t5_bucket(rel):
    nb = NUM_BUCKETS // 2
    max_exact = nb // 2
    bucket = (rel > 0).astype(jnp.int32) * nb
    n = jnp.abs(rel)
    n_f = jnp.maximum(n, 1).astype(jnp.float32)
    large = max_exact + (jnp.log(n_f / max_exact) / math.log(MAX_DISTANCE / max_exact)
                         * (nb - max_exact)).astype(jnp.int32)
    large = jnp.minimum(large, nb - 1)
    return bucket + jnp.where(n < max_exact, n, large)


def windowed_sink_attention(q, k, v, sink, rel_table):
    B, S, Hq, D = q.shape
    Hkv = k.shape[2]
    G = Hq // Hkv
    nb = S // BLOCK
    scale = HEAD_DIM ** -0.5
    qb = q.reshape(B, nb, BLOCK, Hkv, G, D).transpose(1, 0, 2, 3, 4, 5)

    def windows(t):
        tp = jnp.pad(t, ((0, 0), (BLOCK, BLOCK), (0, 0), (0, 0)))
        tb = tp.reshape(B, nb + 2, BLOCK, Hkv, D)
        w = jnp.concatenate([tb[:, :-2], tb[:, 1:-1], tb[:, 2:]], axis=2)
        return w.transpose(1, 0, 2, 3, 4)

    kw = windows(k)
    vw = windows(v)
    i = jnp.arange(BLOCK)[:, None]
    j = jnp.arange(3 * BLOCK)[None, :]
    rel = (j - BLOCK) - i
    in_window = jnp.abs(rel) <= WINDOW
    bias = rel_table[t5_bucket(rel)]
    bias = bias.transpose(2, 0, 1).reshape(Hkv, G, BLOCK, 3 * BLOCK).astype(jnp.float32)
    kpos_base = jnp.arange(3 * BLOCK) - BLOCK
    sink_f = sink.astype(jnp.float32).reshape(Hkv, G, 1, 1)
    blk_ids = jnp.arange(nb)

    def one_block(args):
        qi, ki, vi, bidx = args
        kpos = bidx * BLOCK + kpos_base
        valid = in_window & ((kpos >= 0) & (kpos < S))[None, :]
        s = jnp.einsum('bqhgd,bkhd->bhgqk', qi, ki,
                       preferred_element_type=jnp.float32) * scale + bias
        s = jnp.where(valid, s, NEG_INF)
        m = jnp.maximum(jnp.max(s, axis=-1, keepdims=True), sink_f)
        p = jnp.exp(s - m)
        denom = jnp.sum(p, axis=-1, keepdims=True) + jnp.exp(sink_f - m)
        return jnp.einsum('bhgqk,bkhd->bqhgd', (p / denom).astype(vi.dtype), vi)

    out = lax.map(one_block, (qb, kw, vw, blk_ids))
    return out.transpose(1, 0, 2, 3, 4, 5).reshape(B, S, Hq * D)


def axial_rope_angles(S):
    rows = S // GRID_W
    row = jnp.repeat(jnp.arange(rows), GRID_W).astype(jnp.float32)
    col = jnp.tile(jnp.arange(GRID_W), rows).astype(jnp.float32)
    axis_dim = HEAD_DIM // 2
    inv = ROPE_THETA ** (-jnp.arange(0, axis_dim, 2, dtype=jnp.float32) / axis_dim)
    return jnp.concatenate([row[:, None] * inv, col[:, None] * inv], axis=-1)


def apply_rope(x, ang):
    B, S, H, D = x.shape
    xp = x.reshape(B, S, H, D // 2, 2)
    c = jnp.cos(ang)[None, :, None, :]
    s = jnp.sin(ang)[None, :, None, :]
    x0, x1 = xp[..., 0], xp[..., 1]
    out = jnp.stack([x0 * c - x1 * s, x0 * s + x1 * c], axis=-1)
    return out.reshape(B, S, H, D)


def axial_rope_attention(q, k, v, q_g, k_g):
    B, S, Hq, D = q.shape
    Hkv = k.shape[2]
    G = Hq // Hkv
    nb = S // BLOCK
    scale = HEAD_DIM ** -0.5
    ang = axial_rope_angles(S)
    q = apply_rope(head_rms_norm(q, q_g), ang).astype(v.dtype)
    k = apply_rope(head_rms_norm(k, k_g), ang).astype(v.dtype)
    qb = q.reshape(B, nb, BLOCK, Hkv, G, D).transpose(1, 0, 2, 3, 4, 5)

    def one_block(qi):
        s = jnp.einsum('bqhgd,bkhd->bhgqk', qi, k,
                       preferred_element_type=jnp.float32) * scale
        p = jax.nn.softmax(s, axis=-1)
        return jnp.einsum('bhgqk,bkhd->bqhgd', p.astype(v.dtype), v)

    out = lax.map(one_block, qb)
    return out.transpose(1, 0, 2, 3, 4, 5).reshape(B, S, Hq * D)


def encoder_layer(x, rel_table, w_in, w_out, sink, q_g, k_g, ln1_g, ln1_b,
                  w_gate, w_up, w_down, ln2_g, ln2_b):
    B, S, _ = x.shape
    proj = x @ w_in
    splits = [Q_A, Q_A + KV_A, Q_A + 2 * KV_A, Q_A + 2 * KV_A + Q_B, Q_A + 2 * KV_A + Q_B + KV_B]
    qa, ka, va, qb, kb, vb = jnp.split(proj, splits, axis=-1)
    oa = windowed_sink_attention(qa.reshape(B, S, N_HEADS_A, HEAD_DIM),
                                 ka.reshape(B, S, N_KV_A, HEAD_DIM),
                                 va.reshape(B, S, N_KV_A, HEAD_DIM), sink, rel_table)
    ob = axial_rope_attention(qb.reshape(B, S, N_HEADS_B, HEAD_DIM),
                              kb.reshape(B, S, N_KV_B, HEAD_DIM),
                              vb.reshape(B, S, N_KV_B, HEAD_DIM), q_g, k_g)
    mixed = jnp.concatenate([oa, ob], axis=-1) @ w_out
    x = layer_norm(DEEPNORM_ALPHA * x + mixed, ln1_g, ln1_b)
    h = jax.nn.silu(x @ w_gate) * (x @ w_up)
    x = layer_norm(DEEPNORM_ALPHA * x + h @ w_down, ln2_g, ln2_b)
    return x


def run_trunk(x, rel_table, w_in, w_out, attn_sink, q_norm_g, k_norm_g, ln1_g, ln1_b,
              w_gate, w_up, w_down, ln2_g, ln2_b):
    for l in range(DEPTH):
        x = encoder_layer(x, rel_table, w_in[l], w_out[l], attn_sink[l], q_norm_g[l], k_norm_g[l],
                          ln1_g[l], ln1_b[l], w_gate[l], w_up[l], w_down[l], ln2_g[l], ln2_b[l])
    return x


def setup_inputs(seed: int = 0) -> dict:
    key = jax.random.key(seed)
    ks = jax.random.split(key, 16)
    f32 = jnp.float32
    n = lambda k, shape: jax.random.normal(k, shape, dtype=f32)
    return {
        "x_prompt": n(ks[0], (BATCH, SEQ, D_MODEL)),
        "x_sample": n(ks[1], (DEC_BATCH, DEC_SEQ, D_MODEL)),
        "rel_bias_table": 0.5 * n(ks[2], (NUM_BUCKETS, N_HEADS_A)),
        "w_in": n(ks[3], (DEPTH, D_MODEL, D_IN)) * D_MODEL ** -0.5,
        "w_out": n(ks[4], (DEPTH, MIX_WIDTH, D_MODEL)) * MIX_WIDTH ** -0.5 * DEEPNORM_BETA,
        "attn_sink": 0.5 * n(ks[5], (DEPTH, N_HEADS_A)),
        "q_norm_g": 1.0 + 0.05 * n(ks[6], (DEPTH, HEAD_DIM)),
        "k_norm_g": 1.0 + 0.05 * n(ks[7], (DEPTH, HEAD_DIM)),
        "ln1_g": 1.0 + 0.05 * n(ks[8], (DEPTH, D_MODEL)),
        "ln1_b": 0.02 * n(ks[9], (DEPTH, D_MODEL)),
        "w_gate": n(ks[10], (DEPTH, D_MODEL, D_FF)) * D_MODEL ** -0.5,
        "w_up": n(ks[11], (DEPTH, D_MODEL, D_FF)) * D_MODEL ** -0.5,
        "w_down": n(ks[12], (DEPTH, D_FF, D_MODEL)) * D_FF ** -0.5 * DEEPNORM_BETA,
        "ln2_g": 1.0 + 0.05 * n(ks[13], (DEPTH, D_MODEL)),
        "ln2_b": 0.02 * n(ks[14], (DEPTH, D_MODEL)),
    }


def reference(x_prompt, x_sample, rel_bias_table, w_in, w_out, attn_sink, q_norm_g, k_norm_g,
              ln1_g, ln1_b, w_gate, w_up, w_down, ln2_g, ln2_b):
    y_prompt = run_trunk(x_prompt, rel_bias_table, w_in, w_out, attn_sink, q_norm_g, k_norm_g,
                         ln1_g, ln1_b, w_gate, w_up, w_down, ln2_g, ln2_b)
    y_sample = run_trunk(x_sample, rel_bias_table, w_in, w_out, attn_sink, q_norm_g, k_norm_g,
                         ln1_g, ln1_b, w_gate, w_up, w_down, ln2_g, ln2_b)
    return (y_prompt, y_sample)
```

```python
import functools
import math

import jax
import jax.numpy as jnp
import numpy as np
from jax import lax
from jax.experimental import pallas as pl
from jax.experimental.pallas import tpu as pltpu

F32 = jnp.float32
BF16 = jnp.bfloat16

HEAD_DIM = 64
HALF = HEAD_DIM // 2
N_Q = 8
N_KV = 2
GROUP = N_Q // N_KV
Q_W = N_Q * HEAD_DIM
KV_W = N_KV * HEAD_DIM
WIN = 128
NUM_BUCKETS = 32
MAX_DISTANCE = 128
GRID_W = 64
ROPE_THETA = 10000.0
LN_EPS = 1e-5
RMS_EPS = 1e-6
NEG_INF = -1e30
SCALE = HEAD_DIM ** -0.5
V_ROWS = 80
CH = 512
VMEM_LIMIT = 56 * 1024 * 1024


def _params(n_parallel):
    return pltpu.CompilerParams(
        dimension_semantics=("parallel",) * n_parallel,
        vmem_limit_bytes=VMEM_LIMIT)


def _inproj_kernel(x_ref, w_ref, cos_ref, sin_ref, qg_ref, kg_ref,
                   qa_ref, ka_ref, va_ref, qb_ref, kb_ref, vb_ref):
    tm = x_ref.shape[1]
    xb = x_ref[0].astype(BF16)
    pT = lax.dot_general(w_ref[...], xb, (((1,), (1,)), ((), ())),
                         preferred_element_type=F32)
    o = 0
    qa = pT[o:o + Q_W]; o += Q_W
    ka = pT[o:o + KV_W]; o += KV_W
    va = pT[o:o + KV_W]; o += KV_W
    qb = pT[o:o + Q_W]; o += Q_W
    kb = pT[o:o + KV_W]; o += KV_W
    vb = pT[o:o + KV_W]

    def put_v(v_ref, vT):
        w = v_ref.shape[4]
        first_row = lax.broadcasted_iota(jnp.int32, (V_ROWS - HEAD_DIM, w), 0) == 0
        ones_pad = jnp.where(first_row, 1.0, 0.0).astype(BF16)
        for h in range(N_KV):
            for j in range(tm // w):
                v_ref[0, h, j, 0:HEAD_DIM, :] = (
                    vT[h * HEAD_DIM:(h + 1) * HEAD_DIM, j * w:(j + 1) * w].astype(BF16))
                v_ref[0, h, j, HEAD_DIM:V_ROWS, :] = ones_pad

    def put_k(k_ref, kT):
        kr = kT.T
        for h in range(N_KV):
            k_ref[0, h] = kr[:, h * HEAD_DIM:(h + 1) * HEAD_DIM].astype(BF16)

    qa_ref[0] = (qa * SCALE).astype(BF16)
    put_k(ka_ref, ka)
    put_v(va_ref, va)

    c = cos_ref[...]
    s = sin_ref[...]

    def norm_rope(xT, g_ref, gain_scale):
        ms = jnp.mean(xT * xT, axis=0, keepdims=True)
        y = xT * lax.rsqrt(ms + RMS_EPS) * (g_ref[...] * gain_scale)
        y0 = y[:HALF]
        y1 = y[HALF:]
        return y0 * c - y1 * s, y0 * s + y1 * c

    for h in range(N_Q):
        r0, r1 = norm_rope(qb[h * HEAD_DIM:(h + 1) * HEAD_DIM], qg_ref, SCALE)
        qb_ref[0, h * HEAD_DIM:h * HEAD_DIM + HALF, :] = r0.astype(BF16)
        qb_ref[0, h * HEAD_DIM + HALF:(h + 1) * HEAD_DIM, :] = r1.astype(BF16)
    k_rot = []
    for h in range(N_KV):
        r0, r1 = norm_rope(kb[h * HEAD_DIM:(h + 1) * HEAD_DIM], kg_ref, 1.0)
        k_rot += [r0, r1]
    put_k(kb_ref, jnp.concatenate(k_rot, axis=0))
    put_v(vb_ref, vb)


def _inproj(x, w_inT, cosT, sinT, qg, kg):
    B, S, D = x.shape
    d_in = w_inT.shape[0]
    tm = CH
    grid = (B, S // tm)
    q_shape = jax.ShapeDtypeStruct((B, Q_W, S), BF16)
    k_shape = jax.ShapeDtypeStruct((B, N_KV, S, HEAD_DIM), BF16)
    va_shape = jax.ShapeDtypeStruct((B, N_KV, S // WIN, V_ROWS, WIN), BF16)
    vb_shape = jax.ShapeDtypeStruct((B, N_KV, S // CH, V_ROWS, CH), BF16)
    q_spec = pl.BlockSpec((1, Q_W, tm), lambda b, i: (b, 0, i))
    k_spec = pl.BlockSpec((1, N_KV, tm, HEAD_DIM), lambda b, i: (b, 0, i, 0))
    va_spec = pl.BlockSpec((1, N_KV, tm // WIN, V_ROWS, WIN), lambda b, i: (b, 0, i, 0, 0))
    vb_spec = pl.BlockSpec((1, N_KV, tm // CH, V_ROWS, CH), lambda b, i: (b, 0, i, 0, 0))
    return pl.pallas_call(
        _inproj_kernel,
        grid=grid,
        in_specs=[
            pl.BlockSpec((1, tm, D), lambda b, i: (b, i, 0)),
            pl.BlockSpec((d_in, D), lambda b, i: (0, 0)),
            pl.BlockSpec((HALF, tm), lambda b, i: (0, i)),
            pl.BlockSpec((HALF, tm), lambda b, i: (0, i)),
            pl.BlockSpec((HEAD_DIM, 1), lambda b, i: (0, 0)),
            pl.BlockSpec((HEAD_DIM, 1), lambda b, i: (0, 0)),
        ],
        out_specs=[q_spec, k_spec, va_spec, q_spec, k_spec, vb_spec],
        out_shape=[q_shape, k_shape, va_shape, q_shape, k_shape, vb_shape],
        compiler_params=_params(2),
        name="inproj",
    )(x, w_inT, cosT, sinT, qg, kg)


def _global_attn_kernel(qT_ref, k_ref, vT_ref, o_ref, oT_scr):
    tq = qT_ref.shape[2]
    n_chunks = vT_ref.shape[2]

    for g in range(GROUP):
        qT = qT_ref[0, g * HEAD_DIM:(g + 1) * HEAD_DIM, :]

        def body(c, carry):
            m, acc = carry
            off = pl.multiple_of(c * CH, CH)
            kc = k_ref[0, 0, pl.ds(off, CH), :]
            sT = jnp.dot(kc, qT, preferred_element_type=F32)
            m_new = jnp.maximum(m, jnp.max(sT, axis=0, keepdims=True))
            alpha = jnp.exp(m - m_new)
            p = jnp.exp(sT - m_new).astype(BF16)
            pv = jnp.dot(vT_ref[0, 0, c], p, preferred_element_type=F32)
            return m_new, alpha * acc + pv

        m0 = jnp.full((1, tq), NEG_INF, F32)
        acc0 = jnp.zeros((V_ROWS, tq), F32)
        _, acc = lax.fori_loop(0, n_chunks, body, (m0, acc0))
        oT_scr[g * HEAD_DIM:(g + 1) * HEAD_DIM, :] = (
            acc[:HEAD_DIM] * (1.0 / acc[HEAD_DIM:HEAD_DIM + 1]))
    o_ref[0] = oT_scr[...].T.astype(BF16)


def _global_attn(qT, k, vT):
    B, _, S = qT.shape
    tq = CH
    gw = GROUP * HEAD_DIM
    return pl.pallas_call(
        _global_attn_kernel,
        grid=(B, N_KV, S // tq),
        in_specs=[
            pl.BlockSpec((1, gw, tq), lambda b, h, i: (b, h, i)),
            pl.BlockSpec((1, 1, S, HEAD_DIM), lambda b, h, i: (b, h, 0, 0)),
            pl.BlockSpec((1, 1, S // CH, V_ROWS, CH), lambda b, h, i: (b, h, 0, 0, 0)),
        ],
        out_specs=pl.BlockSpec((1, tq, gw), lambda b, h, i: (b, i, h)),
        out_shape=jax.ShapeDtypeStruct((B, S, Q_W), BF16),
        scratch_shapes=[pltpu.VMEM((gw, tq), F32)],
        compiler_params=_params(3),
        name="global_attn",
    )(qT, k, vT)


def _window_attn_kernel(qT_ref, k_ref, vT_ref, bias_ref, mask_ref, sink_ref, o_ref):
    tq = qT_ref.shape[2]
    nblk = tq // WIN
    n_total = vT_ref.shape[2]
    sink = sink_ref[0]
    for j in range(nblk):
        i = pl.program_id(2) * nblk + j
        qcat = jnp.concatenate(
            [qT_ref[0, g * HEAD_DIM:(g + 1) * HEAD_DIM, j * WIN:(j + 1) * WIN]
             for g in range(GROUP)], axis=1)
        s_parts = []
        kb_idx = []
        for c in range(3):
            kb = i - 1 + c
            kb_c = jnp.clip(kb, 0, n_total - 1)
            kb_idx.append(kb_c)
            kc = k_ref[0, 0, pl.ds(pl.multiple_of(kb_c * WIN, WIN), WIN), :]
            s = jnp.dot(kc, qcat, preferred_element_type=F32) + bias_ref[0, c]
            if c != 1:
                thr = jnp.where(kb == kb_c, 0.5, 2.0)
                s = jnp.where(mask_ref[c] > thr, s, NEG_INF)
            s_parts.append(s)
        m = jnp.maximum(jnp.maximum(jnp.max(s_parts[0], axis=0, keepdims=True),
                                    jnp.max(s_parts[1], axis=0, keepdims=True)),
                        jnp.maximum(jnp.max(s_parts[2], axis=0, keepdims=True), sink))
        acc = None
        for c in range(3):
            p = jnp.exp(s_parts[c] - m).astype(BF16)
            pv = jnp.dot(vT_ref[0, 0, kb_idx[c]], p, preferred_element_type=F32)
            acc = pv if acc is None else acc + pv
        denom = acc[HEAD_DIM:HEAD_DIM + 1] + jnp.exp(sink - m)
        oT = acc[:HEAD_DIM] * (1.0 / denom)
        oT2 = jnp.concatenate([oT[:, g * WIN:(g + 1) * WIN] for g in range(GROUP)], axis=0)
        o_ref[0, j * WIN:(j + 1) * WIN, :] = oT2.T.astype(BF16)


def _window_attn(qT, k, vT, biasT, maskT, sink_row):
    B, _, S = qT.shape
    tq = CH
    gw = GROUP * HEAD_DIM
    gl = GROUP * WIN
    return pl.pallas_call(
        _window_attn_kernel,
        grid=(B, N_KV, S // tq),
        in_specs=[
            pl.BlockSpec((1, gw, tq), lambda b, h, i: (b, h, i)),
            pl.BlockSpec((1, 1, S, HEAD_DIM), lambda b, h, i: (b, h, 0, 0)),
            pl.BlockSpec((1, 1, S // WIN, V_ROWS, WIN), lambda b, h, i: (b, h, 0, 0, 0)),
            pl.BlockSpec((1, 3, WIN, gl), lambda b, h, i: (h, 0, 0, 0)),
            pl.BlockSpec((3, WIN, gl), lambda b, h, i: (0, 0, 0)),
            pl.BlockSpec((1, 1, gl), lambda b, h, i: (h, 0, 0)),
        ],
        out_specs=pl.BlockSpec((1, tq, gw), lambda b, h, i: (b, i, h)),
        out_shape=jax.ShapeDtypeStruct((B, S, Q_W), BF16),
        compiler_params=_params(3),
        name="window_attn",
    )(qT, k, vT, biasT, maskT, sink_row)


def _layer_norm(v, g, b):
    mu = jnp.mean(v, axis=-1, keepdims=True)
    d = v - mu
    var = jnp.mean(d * d, axis=-1, keepdims=True)
    return d * lax.rsqrt(var + LN_EPS) * g + b


def _outproj_kernel(oa_ref, ob_ref, x_ref, wa_ref, wb_ref, g_ref, b_ref, y_ref, *, alpha):
    mixed = (jnp.dot(oa_ref[0], wa_ref[...], preferred_element_type=F32)
             + jnp.dot(ob_ref[0], wb_ref[...], preferred_element_type=F32))
    y_ref[0] = _layer_norm(alpha * x_ref[0] + mixed, g_ref[...], b_ref[...])


def _outproj(oa, ob, x, w_a, w_b, g, b, alpha):
    B, S, D = x.shape
    tm = CH
    row = lambda bb, i: (bb, i, 0)
    const = lambda bb, i: (0, 0)
    return pl.pallas_call(
        functools.partial(_outproj_kernel, alpha=alpha),
        grid=(B, S // tm),
        in_specs=[
            pl.BlockSpec((1, tm, Q_W), row),
            pl.BlockSpec((1, tm, Q_W), row),
            pl.BlockSpec((1, tm, D), row),
            pl.BlockSpec((Q_W, D), const),
            pl.BlockSpec((Q_W, D), const),
            pl.BlockSpec((1, D), const),
            pl.BlockSpec((1, D), const),
        ],
        out_specs=pl.BlockSpec((1, tm, D), row),
        out_shape=jax.ShapeDtypeStruct((B, S, D), F32),
        compiler_params=_params(2),
        name="outproj_ln",
    )(oa, ob, x, w_a, w_b, g, b)


def _ffn_kernel(x_ref, wg_ref, wu_ref, wd_ref, g_ref, b_ref, y_ref, *, alpha, ff_chunk):
    x = x_ref[0]
    xb = x.astype(BF16)
    d_ff = wg_ref.shape[1]
    acc = None
    for c0 in range(0, d_ff, ff_chunk):
        gate = jnp.dot(xb, wg_ref[:, c0:c0 + ff_chunk], preferred_element_type=F32)
        up = jnp.dot(xb, wu_ref[:, c0:c0 + ff_chunk], preferred_element_type=F32)
        h = (gate * jax.nn.sigmoid(gate) * up).astype(BF16)
        part = jnp.dot(h, wd_ref[c0:c0 + ff_chunk, :], preferred_element_type=F32)
        acc = part if acc is None else acc + part
    y_ref[0] = _layer_norm(alpha * x + acc, g_ref[...], b_ref[...])


def _ffn(x, w_gate, w_up, w_down, g, b, alpha):
    B, S, D = x.shape
    d_ff = w_gate.shape[1]
    tm = CH
    ff_chunk = 256
    assert d_ff % ff_chunk == 0
    row = lambda bb, i: (bb, i, 0)
    const = lambda bb, i: (0, 0)
    once = pl.Buffered(1)
    return pl.pallas_call(
        functools.partial(_ffn_kernel, alpha=alpha, ff_chunk=ff_chunk),
        grid=(B, S // tm),
        in_specs=[
            pl.BlockSpec((1, tm, D), row),
            pl.BlockSpec((D, d_ff), const, pipeline_mode=once),
            pl.BlockSpec((D, d_ff), const, pipeline_mode=once),
            pl.BlockSpec((d_ff, D), const, pipeline_mode=once),
            pl.BlockSpec((1, D), const),
            pl.BlockSpec((1, D), const),
        ],
        out_specs=pl.BlockSpec((1, tm, D), row),
        out_shape=jax.ShapeDtypeStruct((B, S, D), F32),
        compiler_params=_params(2),
        name="ffn_ln",
    )(x, w_gate, w_up, w_down, g, b)


def _t5_bucket(rel):
    nb = NUM_BUCKETS // 2
    max_exact = nb // 2
    bucket = (rel > 0).astype(jnp.int32) * nb
    n = jnp.abs(rel)
    n_f = jnp.maximum(n, 1).astype(F32)
    large = max_exact + (jnp.log(n_f / max_exact) / math.log(MAX_DISTANCE / max_exact)
                         * (nb - max_exact)).astype(jnp.int32)
    large = jnp.minimum(large, nb - 1)
    return bucket + jnp.where(n < max_exact, n, large)


def _window_tables(rel_table):
    key = jnp.arange(3 * WIN)[:, None]
    qry = jnp.arange(WIN)[None, :]
    rel = (key - WIN) - qry
    bias = rel_table.astype(F32)[_t5_bucket(rel)]
    bias = bias.reshape(3, WIN, WIN, N_KV, GROUP).transpose(3, 0, 1, 4, 2)
    bias = bias.reshape(N_KV, 3, WIN, GROUP * WIN)
    mask = (jnp.abs(rel) <= WIN).astype(F32).reshape(3, WIN, 1, WIN)
    mask = jnp.broadcast_to(mask, (3, WIN, GROUP, WIN)).reshape(3, WIN, GROUP * WIN)
    return bias, mask


def _rope_tables(S):
    rows = S // GRID_W
    row = jnp.repeat(jnp.arange(rows), GRID_W).astype(F32)
    col = jnp.tile(jnp.arange(GRID_W), rows).astype(F32)
    axis_dim = HEAD_DIM // 2
    inv = ROPE_THETA ** (-jnp.arange(0, axis_dim, 2, dtype=F32) / axis_dim)
    ang = jnp.concatenate([row[:, None] * inv, col[:, None] * inv], axis=-1)
    return jnp.cos(ang).T, jnp.sin(ang).T


def _deinterleave_perm():
    within = np.concatenate([np.arange(0, HEAD_DIM, 2), np.arange(1, HEAD_DIM, 2)])
    return within


def _prep_w_in(w_in_l):
    within = _deinterleave_perm()
    cols = np.arange(w_in_l.shape[1])
    base_qb = Q_W + 2 * KV_W
    for h in range(N_Q + N_KV):
        lo = base_qb + h * HEAD_DIM
        cols[lo:lo + HEAD_DIM] = lo + within
    return w_in_l[:, cols].T.astype(BF16)


def _trunk(x, consts, layers, alpha):
    S = x.shape[1]
    cosT, sinT = _rope_tables(S)
    biasT, maskT = consts
    for lw in layers:
        qa, ka, va, qb, kb, vb = _inproj(x, lw["w_inT"], cosT, sinT, lw["qg"], lw["kg"])
        oa = _window_attn(qa, ka, va, biasT, maskT, lw["sink"])
        ob = _global_attn(qb, kb, vb)
        x = _outproj(oa, ob, x, lw["w_out_a"], lw["w_out_b"], lw["ln1_g"], lw["ln1_b"], alpha)
        x = _ffn(x, lw["w_gate"], lw["w_up"], lw["w_down"], lw["ln2_g"], lw["ln2_b"], alpha)
    return x


def kernel(x_prompt, x_sample, rel_bias_table, w_in, w_out, attn_sink, q_norm_g, k_norm_g,
           ln1_g, ln1_b, w_gate, w_up, w_down, ln2_g, ln2_b):
    depth = w_in.shape[0]
    alpha = (2.0 * depth) ** 0.25
    within = _deinterleave_perm()
    consts = _window_tables(rel_bias_table)
    layers = []
    for l in range(depth):
        sink = attn_sink[l].astype(F32).reshape(N_KV, GROUP, 1)
        sink = jnp.broadcast_to(sink, (N_KV, GROUP, WIN)).reshape(N_KV, 1, GROUP * WIN)
        layers.append(dict(
            w_inT=_prep_w_in(w_in[l]),
            qg=q_norm_g[l].astype(F32)[within].reshape(HEAD_DIM, 1),
            kg=k_norm_g[l].astype(F32)[within].reshape(HEAD_DIM, 1),
            sink=sink,
            w_out_a=w_out[l, :Q_W].astype(BF16),
            w_out_b=w_out[l, Q_W:].astype(BF16),
            ln1_g=ln1_g[l].astype(F32).reshape(1, -1),
            ln1_b=ln1_b[l].astype(F32).reshape(1, -1),
            w_gate=w_gate[l].astype(BF16),
            w_up=w_up[l].astype(BF16),
            w_down=w_down[l].astype(BF16),
            ln2_g=ln2_g[l].astype(F32).reshape(1, -1),
            ln2_b=ln2_b[l].astype(F32).reshape(1, -1),
        ))
    y_prompt = _trunk(x_prompt, consts, layers, alpha)
    y_sample = _trunk(x_sample, consts, layers, alpha)
    return (y_prompt, y_sample)
```

```python
import functools
import math

import jax
import jax.numpy as jnp
import numpy as np
from jax import lax
from jax.experimental import pallas as pl
from jax.experimental.pallas import tpu as pltpu

F32 = jnp.float32
BF16 = jnp.bfloat16

HEAD_DIM = 64
HALF = HEAD_DIM // 2
N_Q = 8
N_KV = 2
GROUP = N_Q // N_KV
Q_W = N_Q * HEAD_DIM
KV_W = N_KV * HEAD_DIM
WIN = 128
NUM_BUCKETS = 32
MAX_DISTANCE = 128
GRID_W = 64
ROPE_THETA = 10000.0
LN_EPS = 1e-5
RMS_EPS = 1e-6
NEG_INF = -1e30
SCALE = HEAD_DIM ** -0.5
LOG2E = math.log2(math.e)
V_ROWS = 80
CH = 512
ITEMS_PER_TRIP = 8
VMEM_LIMIT = 56 * 1024 * 1024


def _params(n_parallel):
    return pltpu.CompilerParams(
        dimension_semantics=("parallel",) * n_parallel,
        vmem_limit_bytes=VMEM_LIMIT)


def _inproj_kernel(x_ref, w_ref, cos_ref, sin_ref, qg_ref, kg_ref,
                   qa_ref, ka_ref, va_ref, qb_ref, kb_ref, vb_ref):
    tm = x_ref.shape[1]
    xb = x_ref[0].astype(BF16)
    pT = lax.dot_general(w_ref[...], xb, (((1,), (1,)), ((), ())),
                         preferred_element_type=F32)
    o = 0
    qa = pT[o:o + Q_W]; o += Q_W
    ka = pT[o:o + KV_W]; o += KV_W
    va = pT[o:o + KV_W]; o += KV_W
    qb = pT[o:o + Q_W]; o += Q_W
    kb = pT[o:o + KV_W]; o += KV_W
    vb = pT[o:o + KV_W]

    def put_v(v_ref, vT):
        w = v_ref.shape[4]
        first_row = lax.broadcasted_iota(jnp.int32, (V_ROWS - HEAD_DIM, w), 0) == 0
        ones_pad = jnp.where(first_row, 1.0, 0.0).astype(BF16)
        for h in range(N_KV):
            for j in range(tm // w):
                v_ref[0, h, j, 0:HEAD_DIM, :] = (
                    vT[h * HEAD_DIM:(h + 1) * HEAD_DIM, j * w:(j + 1) * w].astype(BF16))
                v_ref[0, h, j, HEAD_DIM:V_ROWS, :] = ones_pad

    def put_k(k_ref, kT):
        kr = kT.T
        for h in range(N_KV):
            k_ref[0, h] = kr[:, h * HEAD_DIM:(h + 1) * HEAD_DIM].astype(BF16)

    qa_ref[0] = (qa * SCALE).astype(BF16)
    put_k(ka_ref, ka)
    put_v(va_ref, va)

    c = cos_ref[...]
    s = sin_ref[...]

    def norm_rope(xT, g_ref, gain_scale):
        ms = jnp.mean(xT * xT, axis=0, keepdims=True)
        y = xT * lax.rsqrt(ms + RMS_EPS) * (g_ref[...] * gain_scale)
        y0 = y[:HALF]
        y1 = y[HALF:]
        return y0 * c - y1 * s, y0 * s + y1 * c

    for h in range(N_Q):
        r0, r1 = norm_rope(qb[h * HEAD_DIM:(h + 1) * HEAD_DIM], qg_ref, SCALE * LOG2E)
        qb_ref[0, 0, h * HEAD_DIM:h * HEAD_DIM + HALF, :] = r0.astype(BF16)
        qb_ref[0, 0, h * HEAD_DIM + HALF:(h + 1) * HEAD_DIM, :] = r1.astype(BF16)
    k_rot = []
    for h in range(N_KV):
        r0, r1 = norm_rope(kb[h * HEAD_DIM:(h + 1) * HEAD_DIM], kg_ref, 1.0)
        k_rot += [r0, r1]
    put_k(kb_ref, jnp.concatenate(k_rot, axis=0))
    put_v(vb_ref, vb)


def _inproj(x, w_inT, cosT, sinT, qg, kg):
    B, S, D = x.shape
    d_in = w_inT.shape[0]
    tm = CH
    grid = (B, S // tm)
    q_shape = jax.ShapeDtypeStruct((B, Q_W, S), BF16)
    k_shape = jax.ShapeDtypeStruct((B, N_KV, S, HEAD_DIM), BF16)
    va_shape = jax.ShapeDtypeStruct((B, N_KV, S // WIN, V_ROWS, WIN), BF16)
    vb_shape = jax.ShapeDtypeStruct((B, N_KV, S // CH, V_ROWS, CH), BF16)
    q_spec = pl.BlockSpec((1, Q_W, tm), lambda b, i: (b, 0, i))
    qb_shape = jax.ShapeDtypeStruct((B, S // CH, Q_W, CH), BF16)
    qb_spec = pl.BlockSpec((1, 1, Q_W, CH), lambda b, i: (b, i, 0, 0))
    k_spec = pl.BlockSpec((1, N_KV, tm, HEAD_DIM), lambda b, i: (b, 0, i, 0))
    va_spec = pl.BlockSpec((1, N_KV, tm // WIN, V_ROWS, WIN), lambda b, i: (b, 0, i, 0, 0))
    vb_spec = pl.BlockSpec((1, N_KV, tm // CH, V_ROWS, CH), lambda b, i: (b, 0, i, 0, 0))
    return pl.pallas_call(
        _inproj_kernel,
        grid=grid,
        in_specs=[
            pl.BlockSpec((1, tm, D), lambda b, i: (b, i, 0)),
            pl.BlockSpec((d_in, D), lambda b, i: (0, 0)),
            pl.BlockSpec((HALF, tm), lambda b, i: (0, i)),
            pl.BlockSpec((HALF, tm), lambda b, i: (0, i)),
            pl.BlockSpec((HEAD_DIM, 1), lambda b, i: (0, 0)),
            pl.BlockSpec((HEAD_DIM, 1), lambda b, i: (0, 0)),
        ],
        out_specs=[q_spec, k_spec, va_spec, qb_spec, k_spec, vb_spec],
        out_shape=[q_shape, k_shape, va_shape, qb_shape, k_shape, vb_shape],
        compiler_params=_params(2),
        name="inproj",
    )(x, w_inT, cosT, sinT, qg, kg)


def _global_attn_kernel(qT_ref, k_ref, vT_ref, o_ref, s_scr, p_scr, cmax_scr, alpha_scr, acc_scr):
    nq, _, tq = qT_ref.shape[1:]
    n = vT_ref.shape[2]
    per_tile = GROUP * n
    n_items = nq * per_tile
    R = ITEMS_PER_TRIP
    assert n % R == 0
    trips_per_head = n // R

    def decode(item):
        rem = item % per_tile
        return item // per_tile, rem // n, rem % n

    def head_rows(g):
        return pl.ds(pl.multiple_of(g * HEAD_DIM, HEAD_DIM), HEAD_DIM)

    def scores(slot, item):
        qt, g, c = decode(jnp.minimum(item, n_items - 1))
        kc = k_ref[0, 0, pl.ds(pl.multiple_of(c * CH, CH), CH), :]
        s = jnp.dot(kc, qT_ref[0, qt, head_rows(g), :], preferred_element_type=F32)
        s_scr[slot] = s
        cmax_scr[slot] = jnp.max(s, axis=0, keepdims=True)

    def probs(slot, m_prev):
        m = jnp.maximum(m_prev, cmax_scr[slot])
        alpha_scr[slot] = jnp.exp2(m_prev - m)
        p_scr[slot] = jnp.exp2((s_scr[slot] - m).astype(BF16))
        return m

    def accumulate(slot, item, acc):
        _, _, c = decode(jnp.maximum(item, 0))
        return acc * alpha_scr[slot] + jnp.dot(vT_ref[0, 0, c], p_scr[slot],
                                               preferred_element_type=F32)

    def finish(last_item, acc):
        acc_scr[...] = acc
        qt, g, _ = decode(jnp.maximum(last_item, 0))
        o_ref[0, qt, head_rows(g), :] = (
            acc[:HEAD_DIM] * (1.0 / acc[HEAD_DIM:HEAD_DIM + 1])).astype(BF16)

    def body(u, m):
        t0 = R * u
        m = jnp.where(u % trips_per_head == 0, NEG_INF, m)
        acc = acc_scr[...]
        for j in range(R):
            acc = accumulate(j, t0 - R + j, acc)
            m = probs(j, m)
            scores(j, t0 + R + j)
        finish(t0 - 1, acc)
        return m

    for j in range(R):
        scores(j, j)
    p_scr[...] = jnp.zeros(p_scr.shape, BF16)
    alpha_scr[...] = jnp.ones(alpha_scr.shape, F32)
    acc_scr[...] = jnp.ones(acc_scr.shape, F32)
    lax.fori_loop(0, n_items // R, body, jnp.full((1, tq), NEG_INF, F32))
    acc = acc_scr[...]
    for j in range(R):
        acc = accumulate(j, n_items - R + j, acc)
    finish(n_items - 1, acc)


def _global_attn(qT, k, vT):
    B, nq, _, tq = qT.shape
    S = k.shape[2]
    gw = GROUP * HEAD_DIM
    R = ITEMS_PER_TRIP
    return pl.pallas_call(
        _global_attn_kernel,
        grid=(B, N_KV),
        in_specs=[
            pl.BlockSpec((1, nq, gw, tq), lambda b, h: (b, 0, h, 0)),
            pl.BlockSpec((1, 1, S, HEAD_DIM), lambda b, h: (b, h, 0, 0)),
            pl.BlockSpec((1, 1, S // CH, V_ROWS, CH), lambda b, h: (b, h, 0, 0, 0)),
        ],
        out_specs=pl.BlockSpec((1, nq, gw, tq), lambda b, h: (b, 0, h, 0)),
        out_shape=jax.ShapeDtypeStruct((B, nq, Q_W, tq), BF16),
        scratch_shapes=[pltpu.VMEM((R, CH, tq), F32),
                        pltpu.VMEM((R, CH, tq), BF16),
                        pltpu.VMEM((R, 1, tq), F32),
                        pltpu.VMEM((R, 1, tq), F32),
                        pltpu.VMEM((V_ROWS, tq), F32)],
        compiler_params=_params(2),
        name="global_attn",
    )(qT, k, vT)


def _window_attn_kernel(qT_ref, k_ref, vT_ref, bias_ref, mask_ref, sink_ref, o_ref):
    tq = qT_ref.shape[2]
    nblk = tq // WIN
    n_total = vT_ref.shape[2]
    sink = sink_ref[0]
    for j in range(nblk):
        i = pl.program_id(2) * nblk + j
        qcat = jnp.concatenate(
            [qT_ref[0, g * HEAD_DIM:(g + 1) * HEAD_DIM, j * WIN:(j + 1) * WIN]
             for g in range(GROUP)], axis=1)
        s_parts = []
        kb_idx = []
        for c in range(3):
            kb = i - 1 + c
            kb_c = jnp.clip(kb, 0, n_total - 1)
            kb_idx.append(kb_c)
            kc = k_ref[0, 0, pl.ds(pl.multiple_of(kb_c * WIN, WIN), WIN), :]
            s = jnp.dot(kc, qcat, preferred_element_type=F32) + bias_ref[0, c]
            if c != 1:
                thr = jnp.where(kb == kb_c, 0.5, 2.0)
                s = jnp.where(mask_ref[c] > thr, s, NEG_INF)
            s_parts.append(s)
        m = jnp.maximum(jnp.maximum(jnp.max(s_parts[0], axis=0, keepdims=True),
                                    jnp.max(s_parts[1], axis=0, keepdims=True)),
                        jnp.maximum(jnp.max(s_parts[2], axis=0, keepdims=True), sink))
        acc = None
        for c in range(3):
            p = jnp.exp(s_parts[c] - m).astype(BF16)
            pv = jnp.dot(vT_ref[0, 0, kb_idx[c]], p, preferred_element_type=F32)
            acc = pv if acc is None else acc + pv
        denom = acc[HEAD_DIM:HEAD_DIM + 1] + jnp.exp(sink - m)
        oT = acc[:HEAD_DIM] * (1.0 / denom)
        oT2 = jnp.concatenate([oT[:, g * WIN:(g + 1) * WIN] for g in range(GROUP)], axis=0)
        o_ref[0, j * WIN:(j + 1) * WIN, :] = oT2.T.astype(BF16)


def _window_attn(qT, k, vT, biasT, maskT, sink_row):
    B, _, S = qT.shape
    tq = CH
    gw = GROUP * HEAD_DIM
    gl = GROUP * WIN
    return pl.pallas_call(
        _window_attn_kernel,
        grid=(B, N_KV, S // tq),
        in_specs=[
            pl.BlockSpec((1, gw, tq), lambda b, h, i: (b, h, i)),
            pl.BlockSpec((1, 1, S, HEAD_DIM), lambda b, h, i: (b, h, 0, 0)),
            pl.BlockSpec((1, 1, S // WIN, V_ROWS, WIN), lambda b, h, i: (b, h, 0, 0, 0)),
            pl.BlockSpec((1, 3, WIN, gl), lambda b, h, i: (h, 0, 0, 0)),
            pl.BlockSpec((3, WIN, gl), lambda b, h, i: (0, 0, 0)),
            pl.BlockSpec((1, 1, gl), lambda b, h, i: (h, 0, 0)),
        ],
        out_specs=pl.BlockSpec((1, tq, gw), lambda b, h, i: (b, i, h)),
        out_shape=jax.ShapeDtypeStruct((B, S, Q_W), BF16),
        compiler_params=_params(3),
        name="window_attn",
    )(qT, k, vT, biasT, maskT, sink_row)


def _layer_norm(v, g, b):
    mu = jnp.mean(v, axis=-1, keepdims=True)
    d = v - mu
    var = jnp.mean(d * d, axis=-1, keepdims=True)
    return d * lax.rsqrt(var + LN_EPS) * g + b


def _outproj_kernel(oa_ref, ob_ref, x_ref, wa_ref, wb_ref, g_ref, b_ref, y_ref, *, alpha):
    mixed = (jnp.dot(oa_ref[0], wa_ref[...], preferred_element_type=F32)
             + lax.dot_general(ob_ref[0, 0], wb_ref[...], (((0,), (0,)), ((), ())),
                               preferred_element_type=F32))
    y_ref[0] = _layer_norm(alpha * x_ref[0] + mixed, g_ref[...], b_ref[...])


def _outproj(oa, ob, x, w_a, w_b, g, b, alpha):
    B, S, D = x.shape
    tm = CH
    row = lambda bb, i: (bb, i, 0)
    const = lambda bb, i: (0, 0)
    return pl.pallas_call(
        functools.partial(_outproj_kernel, alpha=alpha),
        grid=(B, S // tm),
        in_specs=[
            pl.BlockSpec((1, tm, Q_W), row),
            pl.BlockSpec((1, 1, Q_W, tm), lambda bb, i: (bb, i, 0, 0)),
            pl.BlockSpec((1, tm, D), row),
            pl.BlockSpec((Q_W, D), const),
            pl.BlockSpec((Q_W, D), const),
            pl.BlockSpec((1, D), const),
            pl.BlockSpec((1, D), const),
        ],
        out_specs=pl.BlockSpec((1, tm, D), row),
        out_shape=jax.ShapeDtypeStruct((B, S, D), F32),
        compiler_params=_params(2),
        name="outproj_ln",
    )(oa, ob, x, w_a, w_b, g, b)


def _ffn_kernel(x_ref, wg_ref, wu_ref, wd_ref, g_ref, b_ref, y_ref, *, alpha, ff_chunk):
    x = x_ref[0]
    xb = x.astype(BF16)
    d_ff = wg_ref.shape[1]
    acc = None
    for c0 in range(0, d_ff, ff_chunk):
        gate = jnp.dot(xb, wg_ref[:, c0:c0 + ff_chunk], preferred_element_type=F32)
        up = jnp.dot(xb, wu_ref[:, c0:c0 + ff_chunk], preferred_element_type=F32)
        h = (gate * jax.nn.sigmoid(gate) * up).astype(BF16)
        part = jnp.dot(h, wd_ref[c0:c0 + ff_chunk, :], preferred_element_type=F32)
        acc = part if acc is None else acc + part
    y_ref[0] = _layer_norm(alpha * x + acc, g_ref[...], b_ref[...])


def _ffn(x, w_gate, w_up, w_down, g, b, alpha):
    B, S, D = x.shape
    d_ff = w_gate.shape[1]
    tm = CH
    ff_chunk = 256
    assert d_ff % ff_chunk == 0
    row = lambda bb, i: (bb, i, 0)
    const = lambda bb, i: (0, 0)
    once = pl.Buffered(1)
    return pl.pallas_call(
        functools.partial(_ffn_kernel, alpha=alpha, ff_chunk=ff_chunk),
        grid=(B, S // tm),
        in_specs=[
            pl.BlockSpec((1, tm, D), row),
            pl.BlockSpec((D, d_ff), const, pipeline_mode=once),
            pl.BlockSpec((D, d_ff), const, pipeline_mode=once),
            pl.BlockSpec((d_ff, D), const, pipeline_mode=once),
            pl.BlockSpec((1, D), const),
            pl.BlockSpec((1, D), const),
        ],
        out_specs=pl.BlockSpec((1, tm, D), row),
        out_shape=jax.ShapeDtypeStruct((B, S, D), F32),
        compiler_params=_params(2),
        name="ffn_ln",
    )(x, w_gate, w_up, w_down, g, b)


def _t5_bucket(rel):
    nb = NUM_BUCKETS // 2
    max_exact = nb // 2
    bucket = (rel > 0).astype(jnp.int32) * nb
    n = jnp.abs(rel)
    n_f = jnp.maximum(n, 1).astype(F32)
    large = max_exact + (jnp.log(n_f / max_exact) / math.log(MAX_DISTANCE / max_exact)
                         * (nb - max_exact)).astype(jnp.int32)
    large = jnp.minimum(large, nb - 1)
    return bucket + jnp.where(n < max_exact, n, large)


def _window_tables(rel_table):
    key = jnp.arange(3 * WIN)[:, None]
    qry = jnp.arange(WIN)[None, :]
    rel = (key - WIN) - qry
    bias = rel_table.astype(F32)[_t5_bucket(rel)]
    bias = bias.reshape(3, WIN, WIN, N_KV, GROUP).transpose(3, 0, 1, 4, 2)
    bias = bias.reshape(N_KV, 3, WIN, GROUP * WIN)
    mask = (jnp.abs(rel) <= WIN).astype(F32).reshape(3, WIN, 1, WIN)
    mask = jnp.broadcast_to(mask, (3, WIN, GROUP, WIN)).reshape(3, WIN, GROUP * WIN)
    return bias, mask


def _rope_tables(S):
    rows = S // GRID_W
    row = jnp.repeat(jnp.arange(rows), GRID_W).astype(F32)
    col = jnp.tile(jnp.arange(GRID_W), rows).astype(F32)
    axis_dim = HEAD_DIM // 2
    inv = ROPE_THETA ** (-jnp.arange(0, axis_dim, 2, dtype=F32) / axis_dim)
    ang = jnp.concatenate([row[:, None] * inv, col[:, None] * inv], axis=-1)
    return jnp.cos(ang).T, jnp.sin(ang).T


def _deinterleave_perm():
    within = np.concatenate([np.arange(0, HEAD_DIM, 2), np.arange(1, HEAD_DIM, 2)])
    return within


def _prep_w_in(w_in_l):
    within = _deinterleave_perm()
    cols = np.arange(w_in_l.shape[1])
    base_qb = Q_W + 2 * KV_W
    for h in range(N_Q + N_KV):
        lo = base_qb + h * HEAD_DIM
        cols[lo:lo + HEAD_DIM] = lo + within
    return w_in_l[:, cols].T.astype(BF16)


def _trunk(x, consts, layers, alpha):
    S = x.shape[1]
    cosT, sinT = _rope_tables(S)
    biasT, maskT = consts
    for lw in layers:
        qa, ka, va, qb, kb, vb = _inproj(x, lw["w_inT"], cosT, sinT, lw["qg"], lw["kg"])
        oa = _window_attn(qa, ka, va, biasT, maskT, lw["sink"])
        ob = _global_attn(qb, kb, vb)
        x = _outproj(oa, ob, x, lw["w_out_a"], lw["w_out_b"], lw["ln1_g"], lw["ln1_b"], alpha)
        x = _ffn(x, lw["w_gate"], lw["w_up"], lw["w_down"], lw["ln2_g"], lw["ln2_b"], alpha)
    return x


def kernel(x_prompt, x_sample, rel_bias_table, w_in, w_out, attn_sink, q_norm_g, k_norm_g,
           ln1_g, ln1_b, w_gate, w_up, w_down, ln2_g, ln2_b):
    depth = w_in.shape[0]
    alpha = (2.0 * depth) ** 0.25
    within = _deinterleave_perm()
    consts = _window_tables(rel_bias_table)
    layers = []
    for l in range(depth):
        sink = attn_sink[l].astype(F32).reshape(N_KV, GROUP, 1)
        sink = jnp.broadcast_to(sink, (N_KV, GROUP, WIN)).reshape(N_KV, 1, GROUP * WIN)
        layers.append(dict(
            w_inT=_prep_w_in(w_in[l]),
            qg=q_norm_g[l].astype(F32)[within].reshape(HEAD_DIM, 1),
            kg=k_norm_g[l].astype(F32)[within].reshape(HEAD_DIM, 1),
            sink=sink,
            w_out_a=w_out[l, :Q_W].astype(BF16),
            w_out_b=w_out[l, Q_W:].astype(BF16),
            ln1_g=ln1_g[l].astype(F32).reshape(1, -1),
            ln1_b=ln1_b[l].astype(F32).reshape(1, -1),
            w_gate=w_gate[l].astype(BF16),
            w_up=w_up[l].astype(BF16),
            w_down=w_down[l].astype(BF16),
            ln2_g=ln2_g[l].astype(F32).reshape(1, -1),
            ln2_b=ln2_b[l].astype(F32).reshape(1, -1),
        ))
    y_prompt = _trunk(x_prompt, consts, layers, alpha)
    y_sample = _trunk(x_sample, consts, layers, alpha)
    return (y_prompt, y_sample)
```

```python
import functools
import math

import jax
import jax.numpy as jnp
import numpy as np
from jax import lax
from jax.experimental import pallas as pl
from jax.experimental.pallas import tpu as pltpu

F32 = jnp.float32
BF16 = jnp.bfloat16

HEAD_DIM = 64
HALF = HEAD_DIM // 2
N_Q = 8
N_KV = 2
GROUP = N_Q // N_KV
Q_W = N_Q * HEAD_DIM
KV_W = N_KV * HEAD_DIM
WIN = 128
NUM_BUCKETS = 32
MAX_DISTANCE = 128
GRID_W = 64
ROPE_THETA = 10000.0
LN_EPS = 1e-5
RMS_EPS = 1e-6
NEG_INF = -1e30
SCALE = HEAD_DIM ** -0.5
LOG2E = math.log2(math.e)
V_ROWS = 80
CH = 512
ITEMS_PER_TRIP = 8
WIN_ITEMS_PER_TRIP = 4
VMEM_LIMIT = 56 * 1024 * 1024


def _params(n_parallel):
    return pltpu.CompilerParams(
        dimension_semantics=("parallel",) * n_parallel,
        vmem_limit_bytes=VMEM_LIMIT)


def _inproj_kernel(x_ref, w_ref, cos_ref, sin_ref, qg_ref, kg_ref,
                   qa_ref, ka_ref, va_ref, qb_ref, kb_ref, vb_ref):
    tm = x_ref.shape[1]
    xb = x_ref[0].astype(BF16)
    pT = lax.dot_general(w_ref[...], xb, (((1,), (1,)), ((), ())),
                         preferred_element_type=F32)
    o = 0
    qa = pT[o:o + Q_W]; o += Q_W
    ka = pT[o:o + KV_W]; o += KV_W
    va = pT[o:o + KV_W]; o += KV_W
    qb = pT[o:o + Q_W]; o += Q_W
    kb = pT[o:o + KV_W]; o += KV_W
    vb = pT[o:o + KV_W]

    def put_v(v_ref, vT):
        w = v_ref.shape[4]
        first_row = lax.broadcasted_iota(jnp.int32, (V_ROWS - HEAD_DIM, w), 0) == 0
        ones_pad = jnp.where(first_row, 1.0, 0.0).astype(BF16)
        for h in range(N_KV):
            for j in range(tm // w):
                v_ref[0, h, j, 0:HEAD_DIM, :] = (
                    vT[h * HEAD_DIM:(h + 1) * HEAD_DIM, j * w:(j + 1) * w].astype(BF16))
                v_ref[0, h, j, HEAD_DIM:V_ROWS, :] = ones_pad

    def put_k(k_ref, kT):
        kr = kT.T
        for h in range(N_KV):
            k_ref[0, h] = kr[:, h * HEAD_DIM:(h + 1) * HEAD_DIM].astype(BF16)

    qa_s = (qa * (SCALE * LOG2E)).astype(BF16)
    for j in range(tm // WIN):
        for h in range(N_KV):
            for g in range(GROUP):
                r0 = (h * GROUP + g) * HEAD_DIM
                qa_ref[0, j, h, :, g * WIN:(g + 1) * WIN] = (
                    qa_s[r0:r0 + HEAD_DIM, j * WIN:(j + 1) * WIN])
    put_k(ka_ref, ka)
    put_v(va_ref, va)

    c = cos_ref[...]
    s = sin_ref[...]

    def norm_rope(xT, g_ref, gain_scale):
        ms = jnp.mean(xT * xT, axis=0, keepdims=True)
        y = xT * lax.rsqrt(ms + RMS_EPS) * (g_ref[...] * gain_scale)
        y0 = y[:HALF]
        y1 = y[HALF:]
        return y0 * c - y1 * s, y0 * s + y1 * c

    for h in range(N_Q):
        r0, r1 = norm_rope(qb[h * HEAD_DIM:(h + 1) * HEAD_DIM], qg_ref, SCALE * LOG2E)
        qb_ref[0, 0, h * HEAD_DIM:h * HEAD_DIM + HALF, :] = r0.astype(BF16)
        qb_ref[0, 0, h * HEAD_DIM + HALF:(h + 1) * HEAD_DIM, :] = r1.astype(BF16)
    k_rot = []
    for h in range(N_KV):
        r0, r1 = norm_rope(kb[h * HEAD_DIM:(h + 1) * HEAD_DIM], kg_ref, 1.0)
        k_rot += [r0, r1]
    put_k(kb_ref, jnp.concatenate(k_rot, axis=0))
    put_v(vb_ref, vb)


def _inproj(x, w_inT, cosT, sinT, qg, kg):
    B, S, D = x.shape
    d_in = w_inT.shape[0]
    tm = CH
    grid = (B, S // tm)
    q_shape = jax.ShapeDtypeStruct((B, S // WIN, N_KV, HEAD_DIM, GROUP * WIN), BF16)
    k_shape = jax.ShapeDtypeStruct((B, N_KV, S, HEAD_DIM), BF16)
    va_shape = jax.ShapeDtypeStruct((B, N_KV, S // WIN, V_ROWS, WIN), BF16)
    vb_shape = jax.ShapeDtypeStruct((B, N_KV, S // CH, V_ROWS, CH), BF16)
    q_spec = pl.BlockSpec((1, tm // WIN, N_KV, HEAD_DIM, GROUP * WIN),
                          lambda b, i: (b, i, 0, 0, 0))
    qb_shape = jax.ShapeDtypeStruct((B, S // CH, Q_W, CH), BF16)
    qb_spec = pl.BlockSpec((1, 1, Q_W, CH), lambda b, i: (b, i, 0, 0))
    k_spec = pl.BlockSpec((1, N_KV, tm, HEAD_DIM), lambda b, i: (b, 0, i, 0))
    va_spec = pl.BlockSpec((1, N_KV, tm // WIN, V_ROWS, WIN), lambda b, i: (b, 0, i, 0, 0))
    vb_spec = pl.BlockSpec((1, N_KV, tm // CH, V_ROWS, CH), lambda b, i: (b, 0, i, 0, 0))
    return pl.pallas_call(
        _inproj_kernel,
        grid=grid,
        in_specs=[
            pl.BlockSpec((1, tm, D), lambda b, i: (b, i, 0)),
            pl.BlockSpec((d_in, D), lambda b, i: (0, 0)),
            pl.BlockSpec((HALF, tm), lambda b, i: (0, i)),
            pl.BlockSpec((HALF, tm), lambda b, i: (0, i)),
            pl.BlockSpec((HEAD_DIM, 1), lambda b, i: (0, 0)),
            pl.BlockSpec((HEAD_DIM, 1), lambda b, i: (0, 0)),
        ],
        out_specs=[q_spec, k_spec, va_spec, qb_spec, k_spec, vb_spec],
        out_shape=[q_shape, k_shape, va_shape, qb_shape, k_shape, vb_shape],
        compiler_params=_params(2),
        name="inproj",
    )(x, w_inT, cosT, sinT, qg, kg)


def _global_attn_kernel(qT_ref, k_ref, vT_ref, o_ref, s_scr, p_scr, cmax_scr, alpha_scr, acc_scr):
    nq, _, tq = qT_ref.shape[1:]
    n = vT_ref.shape[2]
    per_tile = GROUP * n
    n_items = nq * per_tile
    R = ITEMS_PER_TRIP
    assert n % R == 0
    trips_per_head = n // R

    def decode(item):
        rem = item % per_tile
        return item // per_tile, rem // n, rem % n

    def head_rows(g):
        return pl.ds(pl.multiple_of(g * HEAD_DIM, HEAD_DIM), HEAD_DIM)

    def scores(slot, item):
        qt, g, c = decode(jnp.minimum(item, n_items - 1))
        kc = k_ref[0, 0, pl.ds(pl.multiple_of(c * CH, CH), CH), :]
        s = jnp.dot(kc, qT_ref[0, qt, head_rows(g), :], preferred_element_type=F32)
        s_scr[slot] = s
        cmax_scr[slot] = jnp.max(s, axis=0, keepdims=True)

    def probs(slot, m_prev):
        m = jnp.maximum(m_prev, cmax_scr[slot])
        alpha_scr[slot] = jnp.exp2(m_prev - m)
        p_scr[slot] = jnp.exp2((s_scr[slot] - m).astype(BF16))
        return m

    def accumulate(slot, item, acc):
        _, _, c = decode(jnp.maximum(item, 0))
        return acc * alpha_scr[slot] + jnp.dot(vT_ref[0, 0, c], p_scr[slot],
                                               preferred_element_type=F32)

    def finish(last_item, acc):
        acc_scr[...] = acc
        qt, g, _ = decode(jnp.maximum(last_item, 0))
        o_ref[0, qt, head_rows(g), :] = (
            acc[:HEAD_DIM] * (1.0 / acc[HEAD_DIM:HEAD_DIM + 1])).astype(BF16)

    def body(u, m):
        t0 = R * u
        m = jnp.where(u % trips_per_head == 0, NEG_INF, m)
        acc = acc_scr[...]
        for j in range(R):
            acc = accumulate(j, t0 - R + j, acc)
            m = probs(j, m)
            scores(j, t0 + R + j)
        finish(t0 - 1, acc)
        return m

    for j in range(R):
        scores(j, j)
    p_scr[...] = jnp.zeros(p_scr.shape, BF16)
    alpha_scr[...] = jnp.ones(alpha_scr.shape, F32)
    acc_scr[...] = jnp.ones(acc_scr.shape, F32)
    lax.fori_loop(0, n_items // R, body, jnp.full((1, tq), NEG_INF, F32))
    acc = acc_scr[...]
    for j in range(R):
        acc = accumulate(j, n_items - R + j, acc)
    finish(n_items - 1, acc)


def _global_attn(qT, k, vT):
    B, nq, _, tq = qT.shape
    S = k.shape[2]
    gw = GROUP * HEAD_DIM
    R = ITEMS_PER_TRIP
    return pl.pallas_call(
        _global_attn_kernel,
        grid=(B, N_KV),
        in_specs=[
            pl.BlockSpec((1, nq, gw, tq), lambda b, h: (b, 0, h, 0)),
            pl.BlockSpec((1, 1, S, HEAD_DIM), lambda b, h: (b, h, 0, 0)),
            pl.BlockSpec((1, 1, S // CH, V_ROWS, CH), lambda b, h: (b, h, 0, 0, 0)),
        ],
        out_specs=pl.BlockSpec((1, nq, gw, tq), lambda b, h: (b, 0, h, 0)),
        out_shape=jax.ShapeDtypeStruct((B, nq, Q_W, tq), BF16),
        scratch_shapes=[pltpu.VMEM((R, CH, tq), F32),
                        pltpu.VMEM((R, CH, tq), BF16),
                        pltpu.VMEM((R, 1, tq), F32),
                        pltpu.VMEM((R, 1, tq), F32),
                        pltpu.VMEM((V_ROWS, tq), F32)],
        compiler_params=_params(2),
        name="global_attn",
    )(qT, k, vT)


def _window_attn_kernel(q_ref, k_ref, vT_ref, bias_ref, mask_ref, sink_ref, o_ref,
                        s_scr, p_scr, m_scr, sinkp_scr):
    nb = q_ref.shape[1]
    R = WIN_ITEMS_PER_TRIP
    assert nb % R == 0
    n_trips = nb // R
    sink = sink_ref[0]

    def key_block(i, c):
        kb = i - 1 + c
        return kb, jnp.clip(kb, 0, nb - 1)

    def scores(slot, i):
        q = q_ref[0, i, 0]
        m = sink
        for c in range(3):
            kb, kb_c = key_block(i, c)
            kc = k_ref[0, 0, pl.ds(pl.multiple_of(kb_c * WIN, WIN), WIN), :]
            s = jnp.dot(kc, q, preferred_element_type=F32) + bias_ref[0, c]
            if c != 1:
                thr = jnp.where(kb == kb_c, 0.5, 2.0)
                s = jnp.where(mask_ref[c] > thr, s, NEG_INF)
            s_scr[slot, c] = s
            m = jnp.maximum(m, jnp.max(s, axis=0, keepdims=True))
        m_scr[slot] = m

    def probs(slot):
        m = m_scr[slot]
        sinkp_scr[slot] = jnp.exp2(sink - m)
        for c in range(3):
            p_scr[slot, c] = jnp.exp2((s_scr[slot, c] - m).astype(BF16))

    def output(slot, i):
        i = jnp.maximum(i, 0)
        acc = None
        for c in range(3):
            _, kb_c = key_block(i, c)
            pv = jnp.dot(vT_ref[0, 0, kb_c], p_scr[slot, c], preferred_element_type=F32)
            acc = pv if acc is None else acc + pv
        denom = acc[HEAD_DIM:HEAD_DIM + 1] + sinkp_scr[slot]
        oT = (acc[:HEAD_DIM] * (1.0 / denom)).astype(BF16)
        for g in range(GROUP):
            o_ref[0, i, g * HEAD_DIM:(g + 1) * HEAD_DIM, :] = oT[:, g * WIN:(g + 1) * WIN]

    def trip(u, with_scores):
        t0 = R * u
        for j in range(R):
            output(j, t0 - R + j)
            probs(j)
            if with_scores:
                scores(j, t0 + R + j)

    for j in range(R):
        scores(j, j)
    p_scr[...] = jnp.zeros(p_scr.shape, BF16)
    sinkp_scr[...] = jnp.ones(sinkp_scr.shape, F32)

    def body(u, carry):
        trip(u, True)
        return carry

    lax.fori_loop(0, n_trips - 1, body, 0)
    trip(n_trips - 1, False)
    for j in range(R):
        output(j, nb - R + j)


def _window_attn(q, k, vT, biasT, maskT, sink_row):
    B, nb = q.shape[:2]
    S = k.shape[2]
    gw = GROUP * HEAD_DIM
    gl = GROUP * WIN
    R = WIN_ITEMS_PER_TRIP
    return pl.pallas_call(
        _window_attn_kernel,
        grid=(B, N_KV),
        in_specs=[
            pl.BlockSpec((1, nb, 1, HEAD_DIM, gl), lambda b, h: (b, 0, h, 0, 0)),
            pl.BlockSpec((1, 1, S, HEAD_DIM), lambda b, h: (b, h, 0, 0)),
            pl.BlockSpec((1, 1, nb, V_ROWS, WIN), lambda b, h: (b, h, 0, 0, 0)),
            pl.BlockSpec((1, 3, WIN, gl), lambda b, h: (h, 0, 0, 0)),
            pl.BlockSpec((3, WIN, gl), lambda b, h: (0, 0, 0)),
            pl.BlockSpec((1, 1, gl), lambda b, h: (h, 0, 0)),
        ],
        out_specs=pl.BlockSpec((1, nb, gw, WIN), lambda b, h: (b, 0, h, 0)),
        out_shape=jax.ShapeDtypeStruct((B, nb, Q_W, WIN), BF16),
        scratch_shapes=[pltpu.VMEM((R, 3, WIN, gl), F32),
                        pltpu.VMEM((R, 3, WIN, gl), BF16),
                        pltpu.VMEM((R, 1, gl), F32),
                        pltpu.VMEM((R, 1, gl), F32)],
        compiler_params=_params(2),
        name="window_attn",
    )(q, k, vT, biasT, maskT, sink_row)


def _layer_norm(v, g, b):
    mu = jnp.mean(v, axis=-1, keepdims=True)
    d = v - mu
    var = jnp.mean(d * d, axis=-1, keepdims=True)
    return d * lax.rsqrt(var + LN_EPS) * g + b


def _outproj_kernel(oa_ref, ob_ref, x_ref, w_ref, g_ref, b_ref, y_ref, *, alpha):
    oaT = jnp.concatenate([oa_ref[0, j] for j in range(oa_ref.shape[1])], axis=1)
    oT = jnp.concatenate([oaT, ob_ref[0, 0]], axis=0)
    mixed = lax.dot_general(oT, w_ref[...], (((0,), (0,)), ((), ())),
                            preferred_element_type=F32)
    y_ref[0] = _layer_norm(alpha * x_ref[0] + mixed, g_ref[...], b_ref[...])


def _outproj(oa, ob, x, w_out, g, b, alpha):
    B, S, D = x.shape
    tm = CH
    row = lambda bb, i: (bb, i, 0)
    const = lambda bb, i: (0, 0)
    return pl.pallas_call(
        functools.partial(_outproj_kernel, alpha=alpha),
        grid=(B, S // tm),
        in_specs=[
            pl.BlockSpec((1, tm // WIN, Q_W, WIN), lambda bb, i: (bb, i, 0, 0)),
            pl.BlockSpec((1, 1, Q_W, tm), lambda bb, i: (bb, i, 0, 0)),
            pl.BlockSpec((1, tm, D), row),
            pl.BlockSpec((2 * Q_W, D), const),
            pl.BlockSpec((1, D), const),
            pl.BlockSpec((1, D), const),
        ],
        out_specs=pl.BlockSpec((1, tm, D), row),
        out_shape=jax.ShapeDtypeStruct((B, S, D), F32),
        compiler_params=_params(2),
        name="outproj_ln",
    )(oa, ob, x, w_out, g, b)


def _ffn_kernel(x_ref, wg_ref, wu_ref, wd_ref, g_ref, b_ref, y_ref, *, alpha, ff_chunk):
    x = x_ref[0]
    xb = x.astype(BF16)
    d_ff = wg_ref.shape[1]
    acc = None
    for c0 in range(0, d_ff, ff_chunk):
        gate = jnp.dot(xb, wg_ref[:, c0:c0 + ff_chunk], preferred_element_type=F32)
        up = jnp.dot(xb, wu_ref[:, c0:c0 + ff_chunk], preferred_element_type=F32)
        h = (gate * jax.nn.sigmoid(gate) * up).astype(BF16)
        part = jnp.dot(h, wd_ref[c0:c0 + ff_chunk, :], preferred_element_type=F32)
        acc = part if acc is None else acc + part
    y_ref[0] = _layer_norm(alpha * x + acc, g_ref[...], b_ref[...])


def _ffn(x, w_gate, w_up, w_down, g, b, alpha):
    B, S, D = x.shape
    d_ff = w_gate.shape[1]
    tm = CH
    ff_chunk = 256
    assert d_ff % ff_chunk == 0
    row = lambda bb, i: (bb, i, 0)
    const = lambda bb, i: (0, 0)
    once = pl.Buffered(1)
    return pl.pallas_call(
        functools.partial(_ffn_kernel, alpha=alpha, ff_chunk=ff_chunk),
        grid=(B, S // tm),
        in_specs=[
            pl.BlockSpec((1, tm, D), row),
            pl.BlockSpec((D, d_ff), const, pipeline_mode=once),
            pl.BlockSpec((D, d_ff), const, pipeline_mode=once),
            pl.BlockSpec((d_ff, D), const, pipeline_mode=once),
            pl.BlockSpec((1, D), const),
            pl.BlockSpec((1, D), const),
        ],
        out_specs=pl.BlockSpec((1, tm, D), row),
        out_shape=jax.ShapeDtypeStruct((B, S, D), F32),
        compiler_params=_params(2),
        name="ffn_ln",
    )(x, w_gate, w_up, w_down, g, b)


def _t5_bucket(rel):
    nb = NUM_BUCKETS // 2
    max_exact = nb // 2
    bucket = (rel > 0).astype(jnp.int32) * nb
    n = jnp.abs(rel)
    n_f = jnp.maximum(n, 1).astype(F32)
    large = max_exact + (jnp.log(n_f / max_exact) / math.log(MAX_DISTANCE / max_exact)
                         * (nb - max_exact)).astype(jnp.int32)
    large = jnp.minimum(large, nb - 1)
    return bucket + jnp.where(n < max_exact, n, large)


def _window_tables(rel_table):
    key = jnp.arange(3 * WIN)[:, None]
    qry = jnp.arange(WIN)[None, :]
    rel = (key - WIN) - qry
    bias = rel_table.astype(F32)[_t5_bucket(rel)]
    bias = bias.reshape(3, WIN, WIN, N_KV, GROUP).transpose(3, 0, 1, 4, 2)
    bias = bias.reshape(N_KV, 3, WIN, GROUP * WIN) * LOG2E
    mask = (jnp.abs(rel) <= WIN).astype(F32).reshape(3, WIN, 1, WIN)
    mask = jnp.broadcast_to(mask, (3, WIN, GROUP, WIN)).reshape(3, WIN, GROUP * WIN)
    return bias, mask


def _rope_tables(S):
    rows = S // GRID_W
    row = jnp.repeat(jnp.arange(rows), GRID_W).astype(F32)
    col = jnp.tile(jnp.arange(GRID_W), rows).astype(F32)
    axis_dim = HEAD_DIM // 2
    inv = ROPE_THETA ** (-jnp.arange(0, axis_dim, 2, dtype=F32) / axis_dim)
    ang = jnp.concatenate([row[:, None] * inv, col[:, None] * inv], axis=-1)
    return jnp.cos(ang).T, jnp.sin(ang).T


def _deinterleave_perm():
    within = np.concatenate([np.arange(0, HEAD_DIM, 2), np.arange(1, HEAD_DIM, 2)])
    return within


def _prep_w_in(w_in_l):
    within = _deinterleave_perm()
    cols = np.arange(w_in_l.shape[1])
    base_qb = Q_W + 2 * KV_W
    for h in range(N_Q + N_KV):
        lo = base_qb + h * HEAD_DIM
        cols[lo:lo + HEAD_DIM] = lo + within
    return w_in_l[:, cols].T.astype(BF16)


def _trunk(x, consts, layers, alpha):
    S = x.shape[1]
    cosT, sinT = _rope_tables(S)
    biasT, maskT = consts
    for lw in layers:
        qa, ka, va, qb, kb, vb = _inproj(x, lw["w_inT"], cosT, sinT, lw["qg"], lw["kg"])
        oa = _window_attn(qa, ka, va, biasT, maskT, lw["sink"])
        ob = _global_attn(qb, kb, vb)
        x = _outproj(oa, ob, x, lw["w_out"], lw["ln1_g"], lw["ln1_b"], alpha)
        x = _ffn(x, lw["w_gate"], lw["w_up"], lw["w_down"], lw["ln2_g"], lw["ln2_b"], alpha)
    return x


def kernel(x_prompt, x_sample, rel_bias_table, w_in, w_out, attn_sink, q_norm_g, k_norm_g,
           ln1_g, ln1_b, w_gate, w_up, w_down, ln2_g, ln2_b):
    depth = w_in.shape[0]
    alpha = (2.0 * depth) ** 0.25
    within = _deinterleave_perm()
    consts = _window_tables(rel_bias_table)
    layers = []
    for l in range(depth):
        sink = (attn_sink[l].astype(F32) * LOG2E).reshape(N_KV, GROUP, 1)
        sink = jnp.broadcast_to(sink, (N_KV, GROUP, WIN)).reshape(N_KV, 1, GROUP * WIN)
        layers.append(dict(
            w_inT=_prep_w_in(w_in[l]),
            qg=q_norm_g[l].astype(F32)[within].reshape(HEAD_DIM, 1),
            kg=k_norm_g[l].astype(F32)[within].reshape(HEAD_DIM, 1),
            sink=sink,
            w_out=w_out[l].astype(BF16),
            ln1_g=ln1_g[l].astype(F32).reshape(1, -1),
            ln1_b=ln1_b[l].astype(F32).reshape(1, -1),
            w_gate=w_gate[l].astype(BF16),
            w_up=w_up[l].astype(BF16),
            w_down=w_down[l].astype(BF16),
            ln2_g=ln2_g[l].astype(F32).reshape(1, -1),
            ln2_b=ln2_b[l].astype(F32).reshape(1, -1),
        ))
    y_prompt = _trunk(x_prompt, consts, layers, alpha)
    y_sample = _trunk(x_sample, consts, layers, alpha)
    return (y_prompt, y_sample)
```

```python
import functools
import math

import jax
import jax.numpy as jnp
import numpy as np
from jax import lax
from jax.experimental import pallas as pl
from jax.experimental.pallas import tpu as pltpu

F32 = jnp.float32
BF16 = jnp.bfloat16

HEAD_DIM = 64
HALF = HEAD_DIM // 2
N_Q = 8
N_KV = 2
GROUP = N_Q // N_KV
Q_W = N_Q * HEAD_DIM
KV_W = N_KV * HEAD_DIM
WIN = 128
NUM_BUCKETS = 32
MAX_DISTANCE = 128
GRID_W = 64
ROPE_THETA = 10000.0
LN_EPS = 1e-5
RMS_EPS = 1e-6
NEG_INF = -1e30
SCALE = HEAD_DIM ** -0.5
LOG2E = math.log2(math.e)
V_ROWS = 80
CH = 512
ITEMS_PER_TRIP = 16
WIN_ITEMS_PER_TRIP = 4
VMEM_LIMIT = 56 * 1024 * 1024


def _params(n_parallel, flags=None):
    return pltpu.CompilerParams(
        dimension_semantics=("parallel",) * n_parallel,
        vmem_limit_bytes=VMEM_LIMIT, flags=flags)


def _inproj_kernel(x_ref, w_ref, cos_ref, sin_ref, qg_ref, kg_ref,
                   qa_ref, ka_ref, va_ref, qb_ref, kb_ref, vb_ref):
    tm = x_ref.shape[1]
    xb = x_ref[0].astype(BF16)

    def proj(row0, rows):
        return lax.dot_general(w_ref[row0:row0 + rows, :], xb, (((1,), (1,)), ((), ())),
                               preferred_element_type=F32)

    o_qa, o_ka = 0, Q_W
    o_qb = Q_W + 2 * KV_W
    o_kb = o_qb + Q_W

    def put_v(v_ref, vT):
        w = v_ref.shape[4]
        first_row = lax.broadcasted_iota(jnp.int32, (V_ROWS - HEAD_DIM, w), 0) == 0
        ones_pad = jnp.where(first_row, 1.0, 0.0).astype(BF16)
        for h in range(N_KV):
            for j in range(tm // w):
                v_ref[0, h, j, 0:HEAD_DIM, :] = (
                    vT[h * HEAD_DIM:(h + 1) * HEAD_DIM, j * w:(j + 1) * w].astype(BF16))
                v_ref[0, h, j, HEAD_DIM:V_ROWS, :] = ones_pad

    def put_k(k_ref, kT):
        kr = kT.T
        for h in range(N_KV):
            k_ref[0, h] = kr[:, h * HEAD_DIM:(h + 1) * HEAD_DIM].astype(BF16)

    c = cos_ref[...]
    s = sin_ref[...]

    def norm_rope(xT, g_ref, gain_scale):
        ms = jnp.mean(xT * xT, axis=0, keepdims=True)
        y = xT * lax.rsqrt(ms + RMS_EPS) * (g_ref[...] * gain_scale)
        y0 = y[:HALF]
        y1 = y[HALF:]
        return y0 * c - y1 * s, y0 * s + y1 * c

    heads_per_dot = GROUP
    for h0 in range(0, N_Q, heads_per_dot):
        qb = proj(o_qb + h0 * HEAD_DIM, heads_per_dot * HEAD_DIM)
        for hh in range(heads_per_dot):
            h = h0 + hh
            r0, r1 = norm_rope(qb[hh * HEAD_DIM:(hh + 1) * HEAD_DIM], qg_ref, SCALE * LOG2E)
            qb_ref[0, 0, h * HEAD_DIM:h * HEAD_DIM + HALF, :] = r0.astype(BF16)
            qb_ref[0, 0, h * HEAD_DIM + HALF:(h + 1) * HEAD_DIM, :] = r1.astype(BF16)
    kvb = proj(o_kb, 2 * KV_W)
    k_rot = []
    for h in range(N_KV):
        r0, r1 = norm_rope(kvb[h * HEAD_DIM:(h + 1) * HEAD_DIM], kg_ref, 1.0)
        k_rot += [r0, r1]
    put_k(kb_ref, jnp.concatenate(k_rot, axis=0))
    put_v(vb_ref, kvb[KV_W:])

    kva = proj(o_ka, 2 * KV_W)
    put_k(ka_ref, kva[:KV_W])
    put_v(va_ref, kva[KV_W:])
    for h in range(N_KV):
        qa_s = (proj(o_qa + h * GROUP * HEAD_DIM, GROUP * HEAD_DIM)
                * (SCALE * LOG2E)).astype(BF16)
        for j in range(tm // WIN):
            for g in range(GROUP):
                qa_ref[0, j, h, :, g * WIN:(g + 1) * WIN] = (
                    qa_s[g * HEAD_DIM:(g + 1) * HEAD_DIM, j * WIN:(j + 1) * WIN])


def _inproj(x, w_inT, cosT, sinT, qg, kg):
    B, S, D = x.shape
    d_in = w_inT.shape[0]
    tm = CH
    grid = (B, S // tm)
    q_shape = jax.ShapeDtypeStruct((B, S // WIN, N_KV, HEAD_DIM, GROUP * WIN), BF16)
    k_shape = jax.ShapeDtypeStruct((B, N_KV, S, HEAD_DIM), BF16)
    va_shape = jax.ShapeDtypeStruct((B, N_KV, S // WIN, V_ROWS, WIN), BF16)
    vb_shape = jax.ShapeDtypeStruct((B, N_KV, S // CH, V_ROWS, CH), BF16)
    q_spec = pl.BlockSpec((1, tm // WIN, N_KV, HEAD_DIM, GROUP * WIN),
                          lambda b, i: (b, i, 0, 0, 0))
    qb_shape = jax.ShapeDtypeStruct((B, S // CH, Q_W, CH), BF16)
    qb_spec = pl.BlockSpec((1, 1, Q_W, CH), lambda b, i: (b, i, 0, 0))
    k_spec = pl.BlockSpec((1, N_KV, tm, HEAD_DIM), lambda b, i: (b, 0, i, 0))
    va_spec = pl.BlockSpec((1, N_KV, tm // WIN, V_ROWS, WIN), lambda b, i: (b, 0, i, 0, 0))
    vb_spec = pl.BlockSpec((1, N_KV, tm // CH, V_ROWS, CH), lambda b, i: (b, 0, i, 0, 0))
    return pl.pallas_call(
        _inproj_kernel,
        grid=grid,
        in_specs=[
            pl.BlockSpec((1, tm, D), lambda b, i: (b, i, 0)),
            pl.BlockSpec((d_in, D), lambda b, i: (0, 0)),
            pl.BlockSpec((HALF, tm), lambda b, i: (0, i)),
            pl.BlockSpec((HALF, tm), lambda b, i: (0, i)),
            pl.BlockSpec((HEAD_DIM, 1), lambda b, i: (0, 0)),
            pl.BlockSpec((HEAD_DIM, 1), lambda b, i: (0, 0)),
        ],
        out_specs=[q_spec, k_spec, va_spec, qb_spec, k_spec, vb_spec],
        out_shape=[q_shape, k_shape, va_shape, qb_shape, k_shape, vb_shape],
        compiler_params=_params(2),
        name="inproj",
    )(x, w_inT, cosT, sinT, qg, kg)


def _global_attn_kernel(qT_ref, k_ref, vT_ref, o_ref, s_scr, p_scr, cmax_scr, alpha_scr, acc_scr):
    nq, _, tq = qT_ref.shape[1:]
    n = vT_ref.shape[2]
    per_tile = GROUP * n
    n_items = nq * per_tile
    R = s_scr.shape[0]
    assert n % R == 0
    trips_per_head = n // R

    def decode(item):
        rem = item % per_tile
        return item // per_tile, rem // n, rem % n

    def head_rows(g):
        return pl.ds(pl.multiple_of(g * HEAD_DIM, HEAD_DIM), HEAD_DIM)

    def scores(slot, item):
        qt, g, c = decode(jnp.minimum(item, n_items - 1))
        kc = k_ref[0, 0, pl.ds(pl.multiple_of(c * CH, CH), CH), :]
        s = jnp.dot(kc, qT_ref[0, qt, head_rows(g), :], preferred_element_type=F32)
        s_scr[slot] = s
        cmax_scr[slot] = jnp.max(s, axis=0, keepdims=True)

    def probs(slot, m_prev):
        m = jnp.maximum(m_prev, cmax_scr[slot])
        alpha_scr[slot] = jnp.exp2(m_prev - m)
        p_scr[slot] = jnp.exp2((s_scr[slot] - m).astype(BF16))
        return m

    def accumulate(slot, item, acc):
        _, _, c = decode(jnp.maximum(item, 0))
        return acc * alpha_scr[slot] + jnp.dot(vT_ref[0, 0, c], p_scr[slot],
                                               preferred_element_type=F32)

    def finish(last_item, acc):
        acc_scr[...] = acc
        qt, g, _ = decode(jnp.maximum(last_item, 0))
        o_ref[0, qt, head_rows(g), :] = (
            acc[:HEAD_DIM] * (1.0 / acc[HEAD_DIM:HEAD_DIM + 1])).astype(BF16)

    def body(u, m):
        t0 = R * u
        m = jnp.where(u % trips_per_head == 0, NEG_INF, m)
        acc = acc_scr[...]
        for j in range(R):
            acc = accumulate(j, t0 - R + j, acc)
            m = probs(j, m)
            scores(j, t0 + R + j)
        finish(t0 - 1, acc)
        return m

    for j in range(R):
        scores(j, j)
    p_scr[...] = jnp.zeros(p_scr.shape, BF16)
    alpha_scr[...] = jnp.ones(alpha_scr.shape, F32)
    acc_scr[...] = jnp.ones(acc_scr.shape, F32)
    lax.fori_loop(0, n_items // R, body, jnp.full((1, tq), NEG_INF, F32))
    acc = acc_scr[...]
    for j in range(R):
        acc = accumulate(j, n_items - R + j, acc)
    finish(n_items - 1, acc)


def _global_attn(qT, k, vT):
    B, nq, _, tq = qT.shape
    S = k.shape[2]
    gw = GROUP * HEAD_DIM
    R = min(ITEMS_PER_TRIP, S // CH)
    return pl.pallas_call(
        _global_attn_kernel,
        grid=(B, N_KV),
        in_specs=[
            pl.BlockSpec((1, nq, gw, tq), lambda b, h: (b, 0, h, 0)),
            pl.BlockSpec((1, 1, S, HEAD_DIM), lambda b, h: (b, h, 0, 0)),
            pl.BlockSpec((1, 1, S // CH, V_ROWS, CH), lambda b, h: (b, h, 0, 0, 0)),
        ],
        out_specs=pl.BlockSpec((1, nq, gw, tq), lambda b, h: (b, 0, h, 0)),
        out_shape=jax.ShapeDtypeStruct((B, nq, Q_W, tq), BF16),
        scratch_shapes=[pltpu.VMEM((R, CH, tq), F32),
                        pltpu.VMEM((R, CH, tq), BF16),
                        pltpu.VMEM((R, 1, tq), F32),
                        pltpu.VMEM((R, 1, tq), F32),
                        pltpu.VMEM((V_ROWS, tq), F32)],
        compiler_params=_params(2),
        name="global_attn",
    )(qT, k, vT)


def _window_attn_kernel(q_ref, k_ref, vT_ref, bias_ref, mask_ref, sink_ref, o_ref,
                        s_scr, p_scr, m_scr, sinkp_scr):
    nb = q_ref.shape[1]
    R = WIN_ITEMS_PER_TRIP
    assert nb % R == 0
    n_trips = nb // R
    sink = sink_ref[0]

    def key_block(i, c):
        kb = i - 1 + c
        return kb, jnp.clip(kb, 0, nb - 1)

    def scores(slot, i):
        q = q_ref[0, i, 0]
        m = sink
        for c in range(3):
            kb, kb_c = key_block(i, c)
            kc = k_ref[0, 0, pl.ds(pl.multiple_of(kb_c * WIN, WIN), WIN), :]
            s = jnp.dot(kc, q, preferred_element_type=F32) + bias_ref[0, c]
            if c != 1:
                thr = jnp.where(kb == kb_c, 0.5, 2.0)
                s = jnp.where(mask_ref[c] > thr, s, NEG_INF)
            s_scr[slot, c] = s
            m = jnp.maximum(m, jnp.max(s, axis=0, keepdims=True))
        m_scr[slot] = m

    def probs(slot):
        m = m_scr[slot]
        sinkp_scr[slot] = jnp.exp2(sink - m)
        for c in range(3):
            p_scr[slot, c] = jnp.exp2((s_scr[slot, c] - m).astype(BF16))

    def output(slot, i):
        i = jnp.maximum(i, 0)
        acc = None
        for c in range(3):
            _, kb_c = key_block(i, c)
            pv = jnp.dot(vT_ref[0, 0, kb_c], p_scr[slot, c], preferred_element_type=F32)
            acc = pv if acc is None else acc + pv
        denom = acc[HEAD_DIM:HEAD_DIM + 1] + sinkp_scr[slot]
        oT = (acc[:HEAD_DIM] * (1.0 / denom)).astype(BF16)
        for g in range(GROUP):
            o_ref[0, i, g * HEAD_DIM:(g + 1) * HEAD_DIM, :] = oT[:, g * WIN:(g + 1) * WIN]

    def trip(u, with_scores):
        t0 = R * u
        for j in range(R):
            output(j, t0 - R + j)
            probs(j)
            if with_scores:
                scores(j, t0 + R + j)

    for j in range(R):
        scores(j, j)
    p_scr[...] = jnp.zeros(p_scr.shape, BF16)
    sinkp_scr[...] = jnp.ones(sinkp_scr.shape, F32)

    def body(u, carry):
        trip(u, True)
        return carry

    lax.fori_loop(0, n_trips - 1, body, 0)
    trip(n_trips - 1, False)
    for j in range(R):
        output(j, nb - R + j)


def _window_attn(q, k, vT, biasT, maskT, sink_row):
    B, nb = q.shape[:2]
    S = k.shape[2]
    gw = GROUP * HEAD_DIM
    gl = GROUP * WIN
    R = WIN_ITEMS_PER_TRIP
    return pl.pallas_call(
        _window_attn_kernel,
        grid=(B, N_KV),
        in_specs=[
            pl.BlockSpec((1, nb, 1, HEAD_DIM, gl), lambda b, h: (b, 0, h, 0, 0)),
            pl.BlockSpec((1, 1, S, HEAD_DIM), lambda b, h: (b, h, 0, 0)),
            pl.BlockSpec((1, 1, nb, V_ROWS, WIN), lambda b, h: (b, h, 0, 0, 0)),
            pl.BlockSpec((1, 3, WIN, gl), lambda b, h: (h, 0, 0, 0)),
            pl.BlockSpec((3, WIN, gl), lambda b, h: (0, 0, 0)),
            pl.BlockSpec((1, 1, gl), lambda b, h: (h, 0, 0)),
        ],
        out_specs=pl.BlockSpec((1, nb, gw, WIN), lambda b, h: (b, 0, h, 0)),
        out_shape=jax.ShapeDtypeStruct((B, nb, Q_W, WIN), BF16),
        scratch_shapes=[pltpu.VMEM((R, 3, WIN, gl), F32),
                        pltpu.VMEM((R, 3, WIN, gl), BF16),
                        pltpu.VMEM((R, 1, gl), F32),
                        pltpu.VMEM((R, 1, gl), F32)],
        compiler_params=_params(2),
        name="window_attn",
    )(q, k, vT, biasT, maskT, sink_row)


def _layer_norm(v, g, b):
    mu = jnp.mean(v, axis=-1, keepdims=True)
    d = v - mu
    var = jnp.mean(d * d, axis=-1, keepdims=True)
    return d * lax.rsqrt(var + LN_EPS) * g + b


def _mix_ffn_kernel(oa_ref, ob_ref, x_ref, wo_ref, g1_ref, b1_ref, wg_ref, wu_ref, wd_ref,
                    g2_ref, b2_ref, y_ref, *, alpha, ff_chunk):
    oaT = jnp.concatenate([oa_ref[0, j] for j in range(oa_ref.shape[1])], axis=1)
    oT = jnp.concatenate([oaT, ob_ref[0, 0]], axis=0)
    mixed = lax.dot_general(oT, wo_ref[...], (((0,), (0,)), ((), ())),
                            preferred_element_type=F32)
    x1 = _layer_norm(alpha * x_ref[0] + mixed, g1_ref[...], b1_ref[...])
    xb = x1.astype(BF16)
    d_ff = wg_ref.shape[1]
    acc = None
    for c0 in range(0, d_ff, ff_chunk):
        gate = jnp.dot(xb, wg_ref[:, c0:c0 + ff_chunk], preferred_element_type=F32)
        up = jnp.dot(xb, wu_ref[:, c0:c0 + ff_chunk], preferred_element_type=F32)
        h = (gate * jax.nn.sigmoid(gate) * up).astype(BF16)
        part = jnp.dot(h, wd_ref[c0:c0 + ff_chunk, :], preferred_element_type=F32)
        acc = part if acc is None else acc + part
    y_ref[0] = _layer_norm(alpha * x1 + acc, g2_ref[...], b2_ref[...])


def _mix_ffn(oa, ob, x, lw, alpha):
    B, S, D = x.shape
    d_ff = lw["w_gate"].shape[1]
    tm = CH
    ff_chunk = 256
    assert d_ff % ff_chunk == 0
    row = lambda bb, i: (bb, i, 0)
    const = lambda bb, i: (0, 0)
    once = pl.Buffered(1)
    vec = pl.BlockSpec((1, D), const)
    return pl.pallas_call(
        functools.partial(_mix_ffn_kernel, alpha=alpha, ff_chunk=ff_chunk),
        grid=(B, S // tm),
        in_specs=[
            pl.BlockSpec((1, tm // WIN, Q_W, WIN), lambda bb, i: (bb, i, 0, 0)),
            pl.BlockSpec((1, 1, Q_W, tm), lambda bb, i: (bb, i, 0, 0)),
            pl.BlockSpec((1, tm, D), row),
            pl.BlockSpec((2 * Q_W, D), const, pipeline_mode=once),
            vec, vec,
            pl.BlockSpec((D, d_ff), const, pipeline_mode=once),
            pl.BlockSpec((D, d_ff), const, pipeline_mode=once),
            pl.BlockSpec((d_ff, D), const, pipeline_mode=once),
            vec, vec,
        ],
        out_specs=pl.BlockSpec((1, tm, D), row),
        out_shape=jax.ShapeDtypeStruct((B, S, D), F32),
        compiler_params=_params(2),
        name="mix_ffn",
    )(oa, ob, x, lw["w_out"], lw["ln1_g"], lw["ln1_b"], lw["w_gate"], lw["w_up"], lw["w_down"],
      lw["ln2_g"], lw["ln2_b"])


def _t5_bucket(rel):
    nb = NUM_BUCKETS // 2
    max_exact = nb // 2
    bucket = (rel > 0).astype(jnp.int32) * nb
    n = jnp.abs(rel)
    n_f = jnp.maximum(n, 1).astype(F32)
    large = max_exact + (jnp.log(n_f / max_exact) / math.log(MAX_DISTANCE / max_exact)
                         * (nb - max_exact)).astype(jnp.int32)
    large = jnp.minimum(large, nb - 1)
    return bucket + jnp.where(n < max_exact, n, large)


def _window_tables(rel_table):
    key = jnp.arange(3 * WIN)[:, None]
    qry = jnp.arange(WIN)[None, :]
    rel = (key - WIN) - qry
    onehot = (_t5_bucket(rel)[..., None] == jnp.arange(NUM_BUCKETS)).astype(F32)
    bias = jnp.einsum("kqb,bh->kqh", onehot, rel_table.astype(F32),
                      precision=lax.Precision.HIGHEST)
    bias = bias.reshape(3, WIN, WIN, N_KV, GROUP).transpose(3, 0, 1, 4, 2)
    bias = bias.reshape(N_KV, 3, WIN, GROUP * WIN) * LOG2E
    mask = (jnp.abs(rel) <= WIN).astype(F32).reshape(3, WIN, 1, WIN)
    mask = jnp.broadcast_to(mask, (3, WIN, GROUP, WIN)).reshape(3, WIN, GROUP * WIN)
    return bias, mask


def _rope_tables(S):
    rows = S // GRID_W
    row = jnp.repeat(jnp.arange(rows), GRID_W).astype(F32)
    col = jnp.tile(jnp.arange(GRID_W), rows).astype(F32)
    axis_dim = HEAD_DIM // 2
    inv = ROPE_THETA ** (-jnp.arange(0, axis_dim, 2, dtype=F32) / axis_dim)
    ang = jnp.concatenate([row[:, None] * inv, col[:, None] * inv], axis=-1)
    return jnp.cos(ang).T, jnp.sin(ang).T


def _deinterleave_perm():
    within = np.concatenate([np.arange(0, HEAD_DIM, 2), np.arange(1, HEAD_DIM, 2)])
    return within


def _prep_w_in(w_in_l):
    within = _deinterleave_perm()
    cols = np.arange(w_in_l.shape[1])
    base_qb = Q_W + 2 * KV_W
    for h in range(N_Q + N_KV):
        lo = base_qb + h * HEAD_DIM
        cols[lo:lo + HEAD_DIM] = lo + within
    return w_in_l[:, cols].T.astype(BF16)


def _trunk(x, consts, layers, alpha):
    S = x.shape[1]
    cosT, sinT = _rope_tables(S)
    biasT, maskT = consts
    for lw in layers:
        qa, ka, va, qb, kb, vb = _inproj(x, lw["w_inT"], cosT, sinT, lw["qg"], lw["kg"])
        oa = _window_attn(qa, ka, va, biasT, maskT, lw["sink"])
        ob = _global_attn(qb, kb, vb)
        x = _mix_ffn(oa, ob, x, lw, alpha)
    return x


def kernel(x_prompt, x_sample, rel_bias_table, w_in, w_out, attn_sink, q_norm_g, k_norm_g,
           ln1_g, ln1_b, w_gate, w_up, w_down, ln2_g, ln2_b):
    depth = w_in.shape[0]
    alpha = (2.0 * depth) ** 0.25
    within = _deinterleave_perm()
    consts = _window_tables(rel_bias_table)
    layers = []
    for l in range(depth):
        sink = (attn_sink[l].astype(F32) * LOG2E).reshape(N_KV, GROUP, 1)
        sink = jnp.broadcast_to(sink, (N_KV, GROUP, WIN)).reshape(N_KV, 1, GROUP * WIN)
        layers.append(dict(
            w_inT=_prep_w_in(w_in[l]),
            qg=q_norm_g[l].astype(F32)[within].reshape(HEAD_DIM, 1),
            kg=k_norm_g[l].astype(F32)[within].reshape(HEAD_DIM, 1),
            sink=sink,
            w_out=w_out[l].astype(BF16),
            ln1_g=ln1_g[l].astype(F32).reshape(1, -1),
            ln1_b=ln1_b[l].astype(F32).reshape(1, -1),
            w_gate=w_gate[l].astype(BF16),
            w_up=w_up[l].astype(BF16),
            w_down=w_down[l].astype(BF16),
            ln2_g=ln2_g[l].astype(F32).reshape(1, -1),
            ln2_b=ln2_b[l].astype(F32).reshape(1, -1),
        ))
    y_prompt = _trunk(x_prompt, consts, layers, alpha)
    y_sample = _trunk(x_sample, consts, layers, alpha)
    return (y_prompt, y_sample)
```

```python
import functools
import math

import jax
import jax.numpy as jnp
import numpy as np
from jax import lax
from jax.experimental import pallas as pl
from jax.experimental.pallas import tpu as pltpu

F32 = jnp.float32
BF16 = jnp.bfloat16

HEAD_DIM = 64
HALF = HEAD_DIM // 2
N_Q = 8
N_KV = 2
GROUP = N_Q // N_KV
Q_W = N_Q * HEAD_DIM
KV_W = N_KV * HEAD_DIM
WIN = 128
NUM_BUCKETS = 32
MAX_DISTANCE = 128
GRID_W = 64
ROPE_THETA = 10000.0
LN_EPS = 1e-5
RMS_EPS = 1e-6
NEG_INF = -1e30
SCALE = HEAD_DIM ** -0.5
LOG2E = math.log2(math.e)
V_ROWS = 80
CH = 512
ITEMS_PER_TRIP = 16
WIN_ITEMS_PER_TRIP = 8
VMEM_LIMIT = 56 * 1024 * 1024


def _params(n_parallel, flags=None):
    return pltpu.CompilerParams(
        dimension_semantics=("parallel",) * n_parallel,
        vmem_limit_bytes=VMEM_LIMIT, flags=flags)


def _inproj_kernel(x_ref, w_ref, cos_ref, sin_ref, qg_ref, kg_ref,
                   qa_ref, ka_ref, va_ref, qb_ref, kb_ref, vb_ref):
    tm = x_ref.shape[1]
    xb = x_ref[0].astype(BF16)

    def proj(row0, rows):
        return lax.dot_general(w_ref[row0:row0 + rows, :], xb, (((1,), (1,)), ((), ())),
                               preferred_element_type=F32)

    o_qa, o_ka = 0, Q_W
    o_qb = Q_W + 2 * KV_W
    o_kb = o_qb + Q_W

    def put_v(v_ref, vT):
        w = v_ref.shape[4]
        first_row = lax.broadcasted_iota(jnp.int32, (V_ROWS - HEAD_DIM, w), 0) == 0
        ones_pad = jnp.where(first_row, 1.0, 0.0).astype(BF16)
        for h in range(N_KV):
            for j in range(tm // w):
                v_ref[0, h, j, 0:HEAD_DIM, :] = (
                    vT[h * HEAD_DIM:(h + 1) * HEAD_DIM, j * w:(j + 1) * w].astype(BF16))
                v_ref[0, h, j, HEAD_DIM:V_ROWS, :] = ones_pad

    def put_k(k_ref, kT):
        kr = kT.T
        for h in range(N_KV):
            k_ref[0, h] = kr[:, h * HEAD_DIM:(h + 1) * HEAD_DIM].astype(BF16)

    c = cos_ref[...]
    s = sin_ref[...]

    def norm_rope(xT, g_ref, gain_scale):
        ms = jnp.mean(xT * xT, axis=0, keepdims=True)
        y = xT * lax.rsqrt(ms + RMS_EPS) * (g_ref[...] * gain_scale)
        y0 = y[:HALF]
        y1 = y[HALF:]
        return y0 * c - y1 * s, y0 * s + y1 * c

    heads_per_dot = GROUP
    for h0 in range(0, N_Q, heads_per_dot):
        qb = proj(o_qb + h0 * HEAD_DIM, heads_per_dot * HEAD_DIM)
        for hh in range(heads_per_dot):
            h = h0 + hh
            r0, r1 = norm_rope(qb[hh * HEAD_DIM:(hh + 1) * HEAD_DIM], qg_ref, SCALE * LOG2E)
            qb_ref[0, 0, h * HEAD_DIM:h * HEAD_DIM + HALF, :] = r0.astype(BF16)
            qb_ref[0, 0, h * HEAD_DIM + HALF:(h + 1) * HEAD_DIM, :] = r1.astype(BF16)
    kvb = proj(o_kb, 2 * KV_W)
    k_rot = []
    for h in range(N_KV):
        r0, r1 = norm_rope(kvb[h * HEAD_DIM:(h + 1) * HEAD_DIM], kg_ref, 1.0)
        k_rot += [r0, r1]
    put_k(kb_ref, jnp.concatenate(k_rot, axis=0))
    put_v(vb_ref, kvb[KV_W:])

    kva = proj(o_ka, 2 * KV_W)
    put_k(ka_ref, kva[:KV_W])
    put_v(va_ref, kva[KV_W:])
    for h in range(N_KV):
        qa_s = (proj(o_qa + h * GROUP * HEAD_DIM, GROUP * HEAD_DIM)
                * (SCALE * LOG2E)).astype(BF16)
        for j in range(tm // WIN):
            for g in range(GROUP):
                qa_ref[0, j, h, :, g * WIN:(g + 1) * WIN] = (
                    qa_s[g * HEAD_DIM:(g + 1) * HEAD_DIM, j * WIN:(j + 1) * WIN])


def _inproj(x, w_inT, cosT, sinT, qg, kg):
    B, S, D = x.shape
    d_in = w_inT.shape[0]
    tm = CH
    grid = (B, S // tm)
    q_shape = jax.ShapeDtypeStruct((B, S // WIN, N_KV, HEAD_DIM, GROUP * WIN), BF16)
    k_shape = jax.ShapeDtypeStruct((B, N_KV, S, HEAD_DIM), BF16)
    va_shape = jax.ShapeDtypeStruct((B, N_KV, S // WIN, V_ROWS, WIN), BF16)
    vb_shape = jax.ShapeDtypeStruct((B, N_KV, S // CH, V_ROWS, CH), BF16)
    q_spec = pl.BlockSpec((1, tm // WIN, N_KV, HEAD_DIM, GROUP * WIN),
                          lambda b, i: (b, i, 0, 0, 0))
    qb_shape = jax.ShapeDtypeStruct((B, S // CH, Q_W, CH), BF16)
    qb_spec = pl.BlockSpec((1, 1, Q_W, CH), lambda b, i: (b, i, 0, 0))
    k_spec = pl.BlockSpec((1, N_KV, tm, HEAD_DIM), lambda b, i: (b, 0, i, 0))
    va_spec = pl.BlockSpec((1, N_KV, tm // WIN, V_ROWS, WIN), lambda b, i: (b, 0, i, 0, 0))
    vb_spec = pl.BlockSpec((1, N_KV, tm // CH, V_ROWS, CH), lambda b, i: (b, 0, i, 0, 0))
    return pl.pallas_call(
        _inproj_kernel,
        grid=grid,
        in_specs=[
            pl.BlockSpec((1, tm, D), lambda b, i: (b, i, 0)),
            pl.BlockSpec((d_in, D), lambda b, i: (0, 0)),
            pl.BlockSpec((HALF, tm), lambda b, i: (0, i)),
            pl.BlockSpec((HALF, tm), lambda b, i: (0, i)),
            pl.BlockSpec((HEAD_DIM, 1), lambda b, i: (0, 0)),
            pl.BlockSpec((HEAD_DIM, 1), lambda b, i: (0, 0)),
        ],
        out_specs=[q_spec, k_spec, va_spec, qb_spec, k_spec, vb_spec],
        out_shape=[q_shape, k_shape, va_shape, qb_shape, k_shape, vb_shape],
        compiler_params=_params(2),
        name="inproj",
    )(x, w_inT, cosT, sinT, qg, kg)


def _global_attn_kernel(qT_ref, k_ref, vT_ref, o_ref, s_scr, p_scr, cmax_scr, alpha_scr, acc_scr):
    nq, _, tq = qT_ref.shape[1:]
    n = vT_ref.shape[2]
    per_tile = GROUP * n
    n_items = nq * per_tile
    R = s_scr.shape[0]
    assert n % R == 0
    trips_per_head = n // R

    def decode(item):
        rem = item % per_tile
        return item // per_tile, rem // n, rem % n

    def head_rows(g):
        return pl.ds(pl.multiple_of(g * HEAD_DIM, HEAD_DIM), HEAD_DIM)

    def scores(slot, item):
        qt, g, c = decode(item)
        kc = k_ref[0, 0, pl.ds(pl.multiple_of(c * CH, CH), CH), :]
        s = jnp.dot(kc, qT_ref[0, qt, head_rows(g), :], preferred_element_type=F32)
        s_scr[slot] = s
        cmax_scr[slot] = jnp.max(s, axis=0, keepdims=True)

    def probs(slot, m_prev):
        m = jnp.maximum(m_prev, cmax_scr[slot])
        alpha_scr[slot] = jnp.exp2(m_prev - m)
        p_scr[slot] = jnp.exp2((s_scr[slot] - m).astype(BF16))
        return m

    def accumulate(slot, item, acc):
        _, _, c = decode(jnp.maximum(item, 0))
        return acc * alpha_scr[slot] + jnp.dot(vT_ref[0, 0, c], p_scr[slot],
                                               preferred_element_type=F32)

    def finish(last_item, acc):
        acc_scr[...] = acc
        qt, g, _ = decode(jnp.maximum(last_item, 0))
        o_ref[0, qt, head_rows(g), :] = (
            acc[:HEAD_DIM] * (1.0 / acc[HEAD_DIM:HEAD_DIM + 1])).astype(BF16)

    def trip(u, m, with_scores):
        t0 = R * u
        m = jnp.where(u % trips_per_head == 0, NEG_INF, m)
        acc = acc_scr[...]
        for j in range(R):
            acc = accumulate(j, t0 - R + j, acc)
            m = probs(j, m)
            if with_scores:
                scores(j, t0 + R + j)
        finish(t0 - 1, acc)
        return m

    n_trips = n_items // R
    for j in range(R):
        scores(j, j)
    p_scr[...] = jnp.zeros(p_scr.shape, BF16)
    alpha_scr[...] = jnp.ones(alpha_scr.shape, F32)
    acc_scr[...] = jnp.ones(acc_scr.shape, F32)
    m = lax.fori_loop(0, n_trips - 1, lambda u, m: trip(u, m, True),
                      jnp.full((1, tq), NEG_INF, F32))
    trip(n_trips - 1, m, False)
    acc = acc_scr[...]
    for j in range(R):
        acc = accumulate(j, n_items - R + j, acc)
    finish(n_items - 1, acc)


def _global_attn(qT, k, vT):
    B, nq, _, tq = qT.shape
    S = k.shape[2]
    gw = GROUP * HEAD_DIM
    R = min(ITEMS_PER_TRIP, S // CH)
    return pl.pallas_call(
        _global_attn_kernel,
        grid=(B, N_KV),
        in_specs=[
            pl.BlockSpec((1, nq, gw, tq), lambda b, h: (b, 0, h, 0)),
            pl.BlockSpec((1, 1, S, HEAD_DIM), lambda b, h: (b, h, 0, 0)),
            pl.BlockSpec((1, 1, S // CH, V_ROWS, CH), lambda b, h: (b, h, 0, 0, 0)),
        ],
        out_specs=pl.BlockSpec((1, nq, gw, tq), lambda b, h: (b, 0, h, 0)),
        out_shape=jax.ShapeDtypeStruct((B, nq, Q_W, tq), BF16),
        scratch_shapes=[pltpu.VMEM((R, CH, tq), F32),
                        pltpu.VMEM((R, CH, tq), BF16),
                        pltpu.VMEM((R, 1, tq), F32),
                        pltpu.VMEM((R, 1, tq), F32),
                        pltpu.VMEM((V_ROWS, tq), F32)],
        compiler_params=_params(2),
        name="global_attn",
    )(qT, k, vT)


def _window_attn_kernel(q_ref, k_ref, vT_ref, bias_ref, mask_ref, sink_ref, o_ref,
                        s_scr, p_scr, m_scr, sinkp_scr):
    nb = q_ref.shape[1]
    R = WIN_ITEMS_PER_TRIP
    assert nb % R == 0
    n_trips = nb // R
    sink = sink_ref[0]

    def key_block(i, c):
        kb = i - 1 + c
        return kb, jnp.clip(kb, 0, nb - 1)

    def scores(slot, i):
        q = q_ref[0, i, 0]
        m = sink
        for c in range(3):
            kb, kb_c = key_block(i, c)
            kc = k_ref[0, 0, pl.ds(pl.multiple_of(kb_c * WIN, WIN), WIN), :]
            s = jnp.dot(kc, q, preferred_element_type=F32) + bias_ref[0, c]
            if c != 1:
                thr = jnp.where(kb == kb_c, 0.5, 2.0)
                s = jnp.where(mask_ref[c] > thr, s, NEG_INF)
            s_scr[slot, c] = s
            m = jnp.maximum(m, jnp.max(s, axis=0, keepdims=True))
        m_scr[slot] = m

    def probs(slot):
        m = m_scr[slot]
        sinkp_scr[slot] = jnp.exp2(sink - m)
        for c in range(3):
            p_scr[slot, c] = jnp.exp2((s_scr[slot, c] - m).astype(BF16))

    def output(slot, i):
        i = jnp.maximum(i, 0)
        acc = None
        for c in range(3):
            _, kb_c = key_block(i, c)
            pv = jnp.dot(vT_ref[0, 0, kb_c], p_scr[slot, c], preferred_element_type=F32)
            acc = pv if acc is None else acc + pv
        denom = acc[HEAD_DIM:HEAD_DIM + 1] + sinkp_scr[slot]
        oT = (acc[:HEAD_DIM] * (1.0 / denom)).astype(BF16)
        for g in range(GROUP):
            o_ref[0, i, g * HEAD_DIM:(g + 1) * HEAD_DIM, :] = oT[:, g * WIN:(g + 1) * WIN]

    def trip(u, with_scores):
        t0 = R * u
        for j in range(R):
            output(j, t0 - R + j)
            probs(j)
            if with_scores:
                scores(j, t0 + R + j)

    for j in range(R):
        scores(j, j)
    p_scr[...] = jnp.zeros(p_scr.shape, BF16)
    sinkp_scr[...] = jnp.ones(sinkp_scr.shape, F32)

    def body(u, carry):
        trip(u, True)
        return carry

    lax.fori_loop(0, n_trips - 1, body, 0)
    trip(n_trips - 1, False)
    for j in range(R):
        output(j, nb - R + j)


def _window_attn(q, k, vT, biasT, maskT, sink_row):
    B, nb = q.shape[:2]
    S = k.shape[2]
    gw = GROUP * HEAD_DIM
    gl = GROUP * WIN
    R = WIN_ITEMS_PER_TRIP
    return pl.pallas_call(
        _window_attn_kernel,
        grid=(B, N_KV),
        in_specs=[
            pl.BlockSpec((1, nb, 1, HEAD_DIM, gl), lambda b, h: (b, 0, h, 0, 0)),
            pl.BlockSpec((1, 1, S, HEAD_DIM), lambda b, h: (b, h, 0, 0)),
            pl.BlockSpec((1, 1, nb, V_ROWS, WIN), lambda b, h: (b, h, 0, 0, 0)),
            pl.BlockSpec((1, 3, WIN, gl), lambda b, h: (h, 0, 0, 0)),
            pl.BlockSpec((3, WIN, gl), lambda b, h: (0, 0, 0)),
            pl.BlockSpec((1, 1, gl), lambda b, h: (h, 0, 0)),
        ],
        out_specs=pl.BlockSpec((1, nb, gw, WIN), lambda b, h: (b, 0, h, 0)),
        out_shape=jax.ShapeDtypeStruct((B, nb, Q_W, WIN), BF16),
        scratch_shapes=[pltpu.VMEM((R, 3, WIN, gl), F32),
                        pltpu.VMEM((R, 3, WIN, gl), BF16),
                        pltpu.VMEM((R, 1, gl), F32),
                        pltpu.VMEM((R, 1, gl), F32)],
        compiler_params=_params(2),
        name="window_attn",
    )(q, k, vT, biasT, maskT, sink_row)


def _layer_norm(v, g, b):
    mu = jnp.mean(v, axis=-1, keepdims=True)
    d = v - mu
    var = jnp.mean(d * d, axis=-1, keepdims=True)
    return d * lax.rsqrt(var + LN_EPS) * g + b


def _mix_ffn_kernel(oa_ref, ob_ref, x_ref, wo_ref, g1_ref, b1_ref, wg_ref, wu_ref, wd_ref,
                    g2_ref, b2_ref, y_ref, *, alpha, ff_chunk):
    oaT = jnp.concatenate([oa_ref[0, j] for j in range(oa_ref.shape[1])], axis=1)
    oT = jnp.concatenate([oaT, ob_ref[0, 0]], axis=0)
    mixed = lax.dot_general(oT, wo_ref[...], (((0,), (0,)), ((), ())),
                            preferred_element_type=F32)
    x1 = _layer_norm(alpha * x_ref[0] + mixed, g1_ref[...], b1_ref[...])
    xb = x1.astype(BF16)
    d_ff = wg_ref.shape[1]
    acc = None
    for c0 in range(0, d_ff, ff_chunk):
        gate = jnp.dot(xb, wg_ref[:, c0:c0 + ff_chunk], preferred_element_type=F32)
        up = jnp.dot(xb, wu_ref[:, c0:c0 + ff_chunk], preferred_element_type=F32)
        h = (gate * jax.nn.sigmoid(gate) * up).astype(BF16)
        part = jnp.dot(h, wd_ref[c0:c0 + ff_chunk, :], preferred_element_type=F32)
        acc = part if acc is None else acc + part
    y_ref[0] = _layer_norm(alpha * x1 + acc, g2_ref[...], b2_ref[...])


def _mix_ffn(oa, ob, x, lw, alpha):
    B, S, D = x.shape
    d_ff = lw["w_gate"].shape[1]
    tm = CH
    ff_chunk = 256
    assert d_ff % ff_chunk == 0
    row = lambda bb, i: (bb, i, 0)
    const = lambda bb, i: (0, 0)
    once = pl.Buffered(1)
    vec = pl.BlockSpec((1, D), const)
    return pl.pallas_call(
        functools.partial(_mix_ffn_kernel, alpha=alpha, ff_chunk=ff_chunk),
        grid=(B, S // tm),
        in_specs=[
            pl.BlockSpec((1, tm // WIN, Q_W, WIN), lambda bb, i: (bb, i, 0, 0)),
            pl.BlockSpec((1, 1, Q_W, tm), lambda bb, i: (bb, i, 0, 0)),
            pl.BlockSpec((1, tm, D), row),
            pl.BlockSpec((2 * Q_W, D), const, pipeline_mode=once),
            vec, vec,
            pl.BlockSpec((D, d_ff), const, pipeline_mode=once),
            pl.BlockSpec((D, d_ff), const, pipeline_mode=once),
            pl.BlockSpec((d_ff, D), const, pipeline_mode=once),
            vec, vec,
        ],
        out_specs=pl.BlockSpec((1, tm, D), row),
        out_shape=jax.ShapeDtypeStruct((B, S, D), F32),
        compiler_params=_params(2),
        name="mix_ffn",
    )(oa, ob, x, lw["w_out"], lw["ln1_g"], lw["ln1_b"], lw["w_gate"], lw["w_up"], lw["w_down"],
      lw["ln2_g"], lw["ln2_b"])


def _t5_bucket(rel):
    nb = NUM_BUCKETS // 2
    max_exact = nb // 2
    bucket = (rel > 0).astype(jnp.int32) * nb
    n = jnp.abs(rel)
    n_f = jnp.maximum(n, 1).astype(F32)
    large = max_exact + (jnp.log(n_f / max_exact) / math.log(MAX_DISTANCE / max_exact)
                         * (nb - max_exact)).astype(jnp.int32)
    large = jnp.minimum(large, nb - 1)
    return bucket + jnp.where(n < max_exact, n, large)


def _window_tables(rel_table):
    key = jnp.arange(3 * WIN)[:, None]
    qry = jnp.arange(WIN)[None, :]
    rel = (key - WIN) - qry
    onehot = (_t5_bucket(rel)[..., None] == jnp.arange(NUM_BUCKETS)).astype(F32)
    bias = jnp.einsum("kqb,bh->kqh", onehot, rel_table.astype(F32),
                      precision=lax.Precision.HIGHEST)
    bias = bias.reshape(3, WIN, WIN, N_KV, GROUP).transpose(3, 0, 1, 4, 2)
    bias = bias.reshape(N_KV, 3, WIN, GROUP * WIN) * LOG2E
    mask = (jnp.abs(rel) <= WIN).astype(F32).reshape(3, WIN, 1, WIN)
    mask = jnp.broadcast_to(mask, (3, WIN, GROUP, WIN)).reshape(3, WIN, GROUP * WIN)
    return bias, mask


def _rope_tables(S):
    rows = S // GRID_W
    row = jnp.repeat(jnp.arange(rows), GRID_W).astype(F32)
    col = jnp.tile(jnp.arange(GRID_W), rows).astype(F32)
    axis_dim = HEAD_DIM // 2
    inv = ROPE_THETA ** (-jnp.arange(0, axis_dim, 2, dtype=F32) / axis_dim)
    ang = jnp.concatenate([row[:, None] * inv, col[:, None] * inv], axis=-1)
    return jnp.cos(ang).T, jnp.sin(ang).T


def _deinterleave_perm():
    within = np.concatenate([np.arange(0, HEAD_DIM, 2), np.arange(1, HEAD_DIM, 2)])
    return within


def _prep_w_in(w_in_l):
    within = _deinterleave_perm()
    cols = np.arange(w_in_l.shape[1])
    base_qb = Q_W + 2 * KV_W
    for h in range(N_Q + N_KV):
        lo = base_qb + h * HEAD_DIM
        cols[lo:lo + HEAD_DIM] = lo + within
    return w_in_l[:, cols].T.astype(BF16)


def _trunk(x, consts, layers, alpha):
    S = x.shape[1]
    cosT, sinT = _rope_tables(S)
    biasT, maskT = consts
    for lw in layers:
        qa, ka, va, qb, kb, vb = _inproj(x, lw["w_inT"], cosT, sinT, lw["qg"], lw["kg"])
        oa = _window_attn(qa, ka, va, biasT, maskT, lw["sink"])
        ob = _global_attn(qb, kb, vb)
        x = _mix_ffn(oa, ob, x, lw, alpha)
    return x


def kernel(x_prompt, x_sample, rel_bias_table, w_in, w_out, attn_sink, q_norm_g, k_norm_g,
           ln1_g, ln1_b, w_gate, w_up, w_down, ln2_g, ln2_b):
    depth = w_in.shape[0]
    alpha = (2.0 * depth) ** 0.25
    within = _deinterleave_perm()
    consts = _window_tables(rel_bias_table)
    layers = []
    for l in range(depth):
        sink = (attn_sink[l].astype(F32) * LOG2E).reshape(N_KV, GROUP, 1)
        sink = jnp.broadcast_to(sink, (N_KV, GROUP, WIN)).reshape(N_KV, 1, GROUP * WIN)
        layers.append(dict(
            w_inT=_prep_w_in(w_in[l]),
            qg=q_norm_g[l].astype(F32)[within].reshape(HEAD_DIM, 1),
            kg=k_norm_g[l].astype(F32)[within].reshape(HEAD_DIM, 1),
            sink=sink,
            w_out=w_out[l].astype(BF16),
            ln1_g=ln1_g[l].astype(F32).reshape(1, -1),
            ln1_b=ln1_b[l].astype(F32).reshape(1, -1),
            w_gate=w_gate[l].astype(BF16),
            w_up=w_up[l].astype(BF16),
            w_down=w_down[l].astype(BF16),
            ln2_g=ln2_g[l].astype(F32).reshape(1, -1),
            ln2_b=ln2_b[l].astype(F32).reshape(1, -1),
        ))
    y_prompt = _trunk(x_prompt, consts, layers, alpha)
    y_sample = _trunk(x_sample, consts, layers, alpha)
    return (y_prompt, y_sample)
```

```python
import functools
import math

import jax
import jax.numpy as jnp
import numpy as np
from jax import lax
from jax.experimental import pallas as pl
from jax.experimental.pallas import tpu as pltpu

F32 = jnp.float32
BF16 = jnp.bfloat16

HEAD_DIM = 64
HALF = HEAD_DIM // 2
N_Q = 8
N_KV = 2
GROUP = N_Q // N_KV
Q_W = N_Q * HEAD_DIM
KV_W = N_KV * HEAD_DIM
WIN = 128
NUM_BUCKETS = 32
MAX_DISTANCE = 128
GRID_W = 64
ROPE_THETA = 10000.0
LN_EPS = 1e-5
RMS_EPS = 1e-6
NEG_INF = -1e30
SCALE = HEAD_DIM ** -0.5
LOG2E = math.log2(math.e)
V_ROWS = 80
CH = 512
ITEMS_PER_TRIP = 16
WIN_ITEMS_PER_TRIP = 8
VMEM_LIMIT = 56 * 1024 * 1024


def _params(n_parallel, flags=None):
    return pltpu.CompilerParams(
        dimension_semantics=("parallel",) * n_parallel,
        vmem_limit_bytes=VMEM_LIMIT, flags=flags)


def _inproj_kernel(x_ref, w_ref, cos_ref, sin_ref, qg_ref, kg_ref,
                   qa_ref, ka_ref, va_ref, qb_ref, kb_ref, vb_ref):
    tm = x_ref.shape[1]
    xb = x_ref[0].astype(BF16)

    def proj(row0, rows):
        return lax.dot_general(w_ref[row0:row0 + rows, :], xb, (((1,), (1,)), ((), ())),
                               preferred_element_type=F32)

    o_qa, o_ka = 0, Q_W
    o_qb = Q_W + 2 * KV_W
    o_kb = o_qb + Q_W

    def put_v(v_ref, vT):
        w = v_ref.shape[4]
        first_row = lax.broadcasted_iota(jnp.int32, (V_ROWS - HEAD_DIM, w), 0) == 0
        ones_pad = jnp.where(first_row, 1.0, 0.0).astype(BF16)
        for h in range(N_KV):
            for j in range(tm // w):
                v_ref[0, h, j, 0:HEAD_DIM, :] = (
                    vT[h * HEAD_DIM:(h + 1) * HEAD_DIM, j * w:(j + 1) * w].astype(BF16))
                v_ref[0, h, j, HEAD_DIM:V_ROWS, :] = ones_pad

    def put_k(k_ref, kT):
        kr = kT.T
        for h in range(N_KV):
            k_ref[0, h] = kr[:, h * HEAD_DIM:(h + 1) * HEAD_DIM].astype(BF16)

    c = cos_ref[...]
    s = sin_ref[...]

    def norm_rope(xT, g_ref, gain_scale):
        ms = jnp.mean(xT * xT, axis=0, keepdims=True)
        y = xT * lax.rsqrt(ms + RMS_EPS) * (g_ref[...] * gain_scale)
        y0 = y[:HALF]
        y1 = y[HALF:]
        return y0 * c - y1 * s, y0 * s + y1 * c

    heads_per_dot = GROUP
    for h0 in range(0, N_Q, heads_per_dot):
        qb = proj(o_qb + h0 * HEAD_DIM, heads_per_dot * HEAD_DIM)
        for hh in range(heads_per_dot):
            h = h0 + hh
            r0, r1 = norm_rope(qb[hh * HEAD_DIM:(hh + 1) * HEAD_DIM], qg_ref, SCALE * LOG2E)
            qb_ref[0, 0, h * HEAD_DIM:h * HEAD_DIM + HALF, :] = r0.astype(BF16)
            qb_ref[0, 0, h * HEAD_DIM + HALF:(h + 1) * HEAD_DIM, :] = r1.astype(BF16)
    kvb = proj(o_kb, 2 * KV_W)
    k_rot = []
    for h in range(N_KV):
        r0, r1 = norm_rope(kvb[h * HEAD_DIM:(h + 1) * HEAD_DIM], kg_ref, 1.0)
        k_rot += [r0, r1]
    put_k(kb_ref, jnp.concatenate(k_rot, axis=0))
    put_v(vb_ref, kvb[KV_W:])

    kva = proj(o_ka, 2 * KV_W)
    put_k(ka_ref, kva[:KV_W])
    put_v(va_ref, kva[KV_W:])
    for h in range(N_KV):
        qa_s = (proj(o_qa + h * GROUP * HEAD_DIM, GROUP * HEAD_DIM)
                * (SCALE * LOG2E)).astype(BF16)
        for j in range(tm // WIN):
            for g in range(GROUP):
                qa_ref[0, j, h, :, g * WIN:(g + 1) * WIN] = (
                    qa_s[g * HEAD_DIM:(g + 1) * HEAD_DIM, j * WIN:(j + 1) * WIN])


def _inproj(x, w_inT, cosT, sinT, qg, kg):
    B, S, D = x.shape
    d_in = w_inT.shape[0]
    tm = CH
    grid = (B, S // tm)
    q_shape = jax.ShapeDtypeStruct((B, S // WIN, N_KV, HEAD_DIM, GROUP * WIN), BF16)
    k_shape = jax.ShapeDtypeStruct((B, N_KV, S, HEAD_DIM), BF16)
    va_shape = jax.ShapeDtypeStruct((B, N_KV, S // WIN, V_ROWS, WIN), BF16)
    vb_shape = jax.ShapeDtypeStruct((B, N_KV, S // CH, V_ROWS, CH), BF16)
    q_spec = pl.BlockSpec((1, tm // WIN, N_KV, HEAD_DIM, GROUP * WIN),
                          lambda b, i: (b, i, 0, 0, 0))
    qb_shape = jax.ShapeDtypeStruct((B, S // CH, Q_W, CH), BF16)
    qb_spec = pl.BlockSpec((1, 1, Q_W, CH), lambda b, i: (b, i, 0, 0))
    k_spec = pl.BlockSpec((1, N_KV, tm, HEAD_DIM), lambda b, i: (b, 0, i, 0))
    va_spec = pl.BlockSpec((1, N_KV, tm // WIN, V_ROWS, WIN), lambda b, i: (b, 0, i, 0, 0))
    vb_spec = pl.BlockSpec((1, N_KV, tm // CH, V_ROWS, CH), lambda b, i: (b, 0, i, 0, 0))
    return pl.pallas_call(
        _inproj_kernel,
        grid=grid,
        in_specs=[
            pl.BlockSpec((1, tm, D), lambda b, i: (b, i, 0)),
            pl.BlockSpec((d_in, D), lambda b, i: (0, 0)),
            pl.BlockSpec((HALF, tm), lambda b, i: (0, i)),
            pl.BlockSpec((HALF, tm), lambda b, i: (0, i)),
            pl.BlockSpec((HEAD_DIM, 1), lambda b, i: (0, 0)),
            pl.BlockSpec((HEAD_DIM, 1), lambda b, i: (0, 0)),
        ],
        out_specs=[q_spec, k_spec, va_spec, qb_spec, k_spec, vb_spec],
        out_shape=[q_shape, k_shape, va_shape, qb_shape, k_shape, vb_shape],
        compiler_params=_params(2),
        name="inproj",
    )(x, w_inT, cosT, sinT, qg, kg)


def _global_attn_kernel(qT_ref, k_ref, vT_ref, o_ref, s_scr, p_scr, cmax_scr, alpha_scr, acc_scr):
    nq, _, tq = qT_ref.shape[1:]
    n = vT_ref.shape[2]
    per_tile = GROUP * n
    n_items = nq * per_tile
    R = s_scr.shape[0]
    assert n % R == 0 or R % n == 0
    trips_per_head = max(n // R, 1)
    slots_per_head = min(n, R)

    def decode(item):
        rem = item % per_tile
        return item // per_tile, rem // n, rem % n

    def head_rows(g):
        return pl.ds(pl.multiple_of(g * HEAD_DIM, HEAD_DIM), HEAD_DIM)

    def scores(slot, item):
        qt, g, c = decode(item)
        kc = k_ref[0, 0, pl.ds(pl.multiple_of(c * CH, CH), CH), :]
        s = jnp.dot(kc, qT_ref[0, qt, head_rows(g), :], preferred_element_type=F32)
        s_scr[slot] = s
        cmax_scr[slot] = jnp.max(s, axis=0, keepdims=True)

    def probs(slot, m_prev):
        m = jnp.maximum(m_prev, cmax_scr[slot])
        alpha_scr[slot] = jnp.exp2(m_prev - m)
        p_scr[slot] = jnp.exp2((s_scr[slot] - m).astype(BF16))
        return m

    def accumulate(slot, item, acc):
        _, _, c = decode(jnp.maximum(item, 0))
        return acc * alpha_scr[slot] + jnp.dot(vT_ref[0, 0, c], p_scr[slot],
                                               preferred_element_type=F32)

    def emit(slot, item, acc):
        if (slot + 1) % slots_per_head == 0:
            qt, g, _ = decode(jnp.maximum(item, 0))
            o_ref[0, qt, head_rows(g), :] = (
                acc[:HEAD_DIM] * (1.0 / acc[HEAD_DIM:HEAD_DIM + 1])).astype(BF16)

    def trip(u, m, with_scores):
        t0 = R * u
        acc = acc_scr[...]
        for j in range(R):
            if j % slots_per_head == 0:
                m = jnp.where(u % trips_per_head == 0, NEG_INF, m)
            acc = accumulate(j, t0 - R + j, acc)
            emit(j, t0 - R + j, acc)
            m = probs(j, m)
            if with_scores:
                scores(j, t0 + R + j)
        acc_scr[...] = acc
        return m

    n_trips = n_items // R
    for j in range(R):
        scores(j, j)
    p_scr[...] = jnp.zeros(p_scr.shape, BF16)
    alpha_scr[...] = jnp.ones(alpha_scr.shape, F32)
    acc_scr[...] = jnp.ones(acc_scr.shape, F32)
    m = lax.fori_loop(0, n_trips - 1, lambda u, m: trip(u, m, True),
                      jnp.full((1, tq), NEG_INF, F32))
    trip(n_trips - 1, m, False)
    acc = acc_scr[...]
    for j in range(R):
        acc = accumulate(j, n_items - R + j, acc)
        emit(j, n_items - R + j, acc)


def _global_attn(qT, k, vT):
    B, nq, _, tq = qT.shape
    S = k.shape[2]
    gw = GROUP * HEAD_DIM
    R = ITEMS_PER_TRIP
    return pl.pallas_call(
        _global_attn_kernel,
        grid=(B, N_KV),
        in_specs=[
            pl.BlockSpec((1, nq, gw, tq), lambda b, h: (b, 0, h, 0)),
            pl.BlockSpec((1, 1, S, HEAD_DIM), lambda b, h: (b, h, 0, 0)),
            pl.BlockSpec((1, 1, S // CH, V_ROWS, CH), lambda b, h: (b, h, 0, 0, 0)),
        ],
        out_specs=pl.BlockSpec((1, nq, gw, tq), lambda b, h: (b, 0, h, 0)),
        out_shape=jax.ShapeDtypeStruct((B, nq, Q_W, tq), BF16),
        scratch_shapes=[pltpu.VMEM((R, CH, tq), F32),
                        pltpu.VMEM((R, CH, tq), BF16),
                        pltpu.VMEM((R, 1, tq), F32),
                        pltpu.VMEM((R, 1, tq), F32),
                        pltpu.VMEM((V_ROWS, tq), F32)],
        compiler_params=_params(2),
        name="global_attn",
    )(qT, k, vT)


def _window_attn_kernel(q_ref, k_ref, vT_ref, bias_ref, mask_ref, sink_ref, o_ref,
                        s_scr, p_scr, m_scr, sinkp_scr):
    nb = q_ref.shape[1]
    R = WIN_ITEMS_PER_TRIP
    assert nb % R == 0
    n_trips = nb // R
    sink = sink_ref[0]

    def key_block(i, c):
        kb = i - 1 + c
        return kb, jnp.clip(kb, 0, nb - 1)

    def scores(slot, i):
        q = q_ref[0, i, 0]
        m = sink
        for c in range(3):
            kb, kb_c = key_block(i, c)
            kc = k_ref[0, 0, pl.ds(pl.multiple_of(kb_c * WIN, WIN), WIN), :]
            s = jnp.dot(kc, q, preferred_element_type=F32) + bias_ref[0, c]
            if c != 1:
                thr = jnp.where(kb == kb_c, 0.5, 2.0)
                s = jnp.where(mask_ref[c] > thr, s, NEG_INF)
            s_scr[slot, c] = s
            m = jnp.maximum(m, jnp.max(s, axis=0, keepdims=True))
        m_scr[slot] = m

    def probs(slot):
        m = m_scr[slot]
        sinkp_scr[slot] = jnp.exp2(sink - m)
        for c in range(3):
            p_scr[slot, c] = jnp.exp2((s_scr[slot, c] - m).astype(BF16))

    def output(slot, i):
        i = jnp.maximum(i, 0)
        acc = None
        for c in range(3):
            _, kb_c = key_block(i, c)
            pv = jnp.dot(vT_ref[0, 0, kb_c], p_scr[slot, c], preferred_element_type=F32)
            acc = pv if acc is None else acc + pv
        denom = acc[HEAD_DIM:HEAD_DIM + 1] + sinkp_scr[slot]
        oT = (acc[:HEAD_DIM] * (1.0 / denom)).astype(BF16)
        for g in range(GROUP):
            o_ref[0, i, g * HEAD_DIM:(g + 1) * HEAD_DIM, :] = oT[:, g * WIN:(g + 1) * WIN]

    def trip(u, with_scores):
        t0 = R * u
        for j in range(R):
            output(j, t0 - R + j)
            probs(j)
            if with_scores:
                scores(j, t0 + R + j)

    for j in range(R):
        scores(j, j)
    p_scr[...] = jnp.zeros(p_scr.shape, BF16)
    sinkp_scr[...] = jnp.ones(sinkp_scr.shape, F32)

    def body(u, carry):
        trip(u, True)
        return carry

    lax.fori_loop(0, n_trips - 1, body, 0)
    trip(n_trips - 1, False)
    for j in range(R):
        output(j, nb - R + j)


def _window_attn(q, k, vT, biasT, maskT, sink_row):
    B, nb = q.shape[:2]
    S = k.shape[2]
    gw = GROUP * HEAD_DIM
    gl = GROUP * WIN
    R = WIN_ITEMS_PER_TRIP
    return pl.pallas_call(
        _window_attn_kernel,
        grid=(B, N_KV),
        in_specs=[
            pl.BlockSpec((1, nb, 1, HEAD_DIM, gl), lambda b, h: (b, 0, h, 0, 0)),
            pl.BlockSpec((1, 1, S, HEAD_DIM), lambda b, h: (b, h, 0, 0)),
            pl.BlockSpec((1, 1, nb, V_ROWS, WIN), lambda b, h: (b, h, 0, 0, 0)),
            pl.BlockSpec((1, 3, WIN, gl), lambda b, h: (h, 0, 0, 0)),
            pl.BlockSpec((3, WIN, gl), lambda b, h: (0, 0, 0)),
            pl.BlockSpec((1, 1, gl), lambda b, h: (h, 0, 0)),
        ],
        out_specs=pl.BlockSpec((1, nb, gw, WIN), lambda b, h: (b, 0, h, 0)),
        out_shape=jax.ShapeDtypeStruct((B, nb, Q_W, WIN), BF16),
        scratch_shapes=[pltpu.VMEM((R, 3, WIN, gl), F32),
                        pltpu.VMEM((R, 3, WIN, gl), BF16),
                        pltpu.VMEM((R, 1, gl), F32),
                        pltpu.VMEM((R, 1, gl), F32)],
        compiler_params=_params(2),
        name="window_attn",
    )(q, k, vT, biasT, maskT, sink_row)


def _layer_norm(v, g, b):
    mu = jnp.mean(v, axis=-1, keepdims=True)
    d = v - mu
    var = jnp.mean(d * d, axis=-1, keepdims=True)
    return d * lax.rsqrt(var + LN_EPS) * g + b


def _mix_ffn_kernel(oa_ref, ob_ref, x_ref, wo_ref, g1_ref, b1_ref, wg_ref, wu_ref, wd_ref,
                    g2_ref, b2_ref, y_ref, *, alpha, ff_chunk):
    oaT = jnp.concatenate([oa_ref[0, j] for j in range(oa_ref.shape[1])], axis=1)
    oT = jnp.concatenate([oaT, ob_ref[0, 0]], axis=0)
    mixed = lax.dot_general(oT, wo_ref[...], (((0,), (0,)), ((), ())),
                            preferred_element_type=F32)
    x1 = _layer_norm(alpha * x_ref[0] + mixed, g1_ref[...], b1_ref[...])
    xb = x1.astype(BF16)
    d_ff = wg_ref.shape[1]
    acc = None
    for c0 in range(0, d_ff, ff_chunk):
        gate = jnp.dot(xb, wg_ref[:, c0:c0 + ff_chunk], preferred_element_type=F32)
        up = jnp.dot(xb, wu_ref[:, c0:c0 + ff_chunk], preferred_element_type=F32)
        h = (gate * jax.nn.sigmoid(gate) * up).astype(BF16)
        part = jnp.dot(h, wd_ref[c0:c0 + ff_chunk, :], preferred_element_type=F32)
        acc = part if acc is None else acc + part
    y_ref[0] = _layer_norm(alpha * x1 + acc, g2_ref[...], b2_ref[...])


def _mix_ffn(oa, ob, x, lw, alpha):
    B, S, D = x.shape
    d_ff = lw["w_gate"].shape[1]
    tm = CH
    ff_chunk = 256
    assert d_ff % ff_chunk == 0
    row = lambda bb, i: (bb, i, 0)
    const = lambda bb, i: (0, 0)
    once = pl.Buffered(1)
    vec = pl.BlockSpec((1, D), const)
    return pl.pallas_call(
        functools.partial(_mix_ffn_kernel, alpha=alpha, ff_chunk=ff_chunk),
        grid=(B, S // tm),
        in_specs=[
            pl.BlockSpec((1, tm // WIN, Q_W, WIN), lambda bb, i: (bb, i, 0, 0)),
            pl.BlockSpec((1, 1, Q_W, tm), lambda bb, i: (bb, i, 0, 0)),
            pl.BlockSpec((1, tm, D), row),
            pl.BlockSpec((2 * Q_W, D), const, pipeline_mode=once),
            vec, vec,
            pl.BlockSpec((D, d_ff), const, pipeline_mode=once),
            pl.BlockSpec((D, d_ff), const, pipeline_mode=once),
            pl.BlockSpec((d_ff, D), const, pipeline_mode=once),
            vec, vec,
        ],
        out_specs=pl.BlockSpec((1, tm, D), row),
        out_shape=jax.ShapeDtypeStruct((B, S, D), F32),
        compiler_params=_params(2),
        name="mix_ffn",
    )(oa, ob, x, lw["w_out"], lw["ln1_g"], lw["ln1_b"], lw["w_gate"], lw["w_up"], lw["w_down"],
      lw["ln2_g"], lw["ln2_b"])


def _t5_bucket(rel):
    nb = NUM_BUCKETS // 2
    max_exact = nb // 2
    bucket = (rel > 0).astype(jnp.int32) * nb
    n = jnp.abs(rel)
    n_f = jnp.maximum(n, 1).astype(F32)
    large = max_exact + (jnp.log(n_f / max_exact) / math.log(MAX_DISTANCE / max_exact)
                         * (nb - max_exact)).astype(jnp.int32)
    large = jnp.minimum(large, nb - 1)
    return bucket + jnp.where(n < max_exact, n, large)


def _window_tables(rel_table):
    key = jnp.arange(3 * WIN)[:, None]
    qry = jnp.arange(WIN)[None, :]
    rel = (key - WIN) - qry
    onehot = (_t5_bucket(rel)[..., None] == jnp.arange(NUM_BUCKETS)).astype(F32)
    bias = jnp.einsum("kqb,bh->kqh", onehot, rel_table.astype(F32),
                      precision=lax.Precision.HIGHEST)
    bias = bias.reshape(3, WIN, WIN, N_KV, GROUP).transpose(3, 0, 1, 4, 2)
    bias = bias.reshape(N_KV, 3, WIN, GROUP * WIN) * LOG2E
    mask = (jnp.abs(rel) <= WIN).astype(F32).reshape(3, WIN, 1, WIN)
    mask = jnp.broadcast_to(mask, (3, WIN, GROUP, WIN)).reshape(3, WIN, GROUP * WIN)
    return bias, mask


def _rope_tables(S):
    rows = S // GRID_W
    row = jnp.repeat(jnp.arange(rows), GRID_W).astype(F32)
    col = jnp.tile(jnp.arange(GRID_W), rows).astype(F32)
    axis_dim = HEAD_DIM // 2
    inv = ROPE_THETA ** (-jnp.arange(0, axis_dim, 2, dtype=F32) / axis_dim)
    ang = jnp.concatenate([row[:, None] * inv, col[:, None] * inv], axis=-1)
    return jnp.cos(ang).T, jnp.sin(ang).T


def _deinterleave_perm():
    within = np.concatenate([np.arange(0, HEAD_DIM, 2), np.arange(1, HEAD_DIM, 2)])
    return within


def _prep_w_in(w_in_l):
    within = _deinterleave_perm()
    cols = np.arange(w_in_l.shape[1])
    base_qb = Q_W + 2 * KV_W
    for h in range(N_Q + N_KV):
        lo = base_qb + h * HEAD_DIM
        cols[lo:lo + HEAD_DIM] = lo + within
    return w_in_l[:, cols].T.astype(BF16)


def _trunk(x, consts, layers, alpha):
    S = x.shape[1]
    cosT, sinT = _rope_tables(S)
    biasT, maskT = consts
    for lw in layers:
        qa, ka, va, qb, kb, vb = _inproj(x, lw["w_inT"], cosT, sinT, lw["qg"], lw["kg"])
        oa = _window_attn(qa, ka, va, biasT, maskT, lw["sink"])
        ob = _global_attn(qb, kb, vb)
        x = _mix_ffn(oa, ob, x, lw, alpha)
    return x


def kernel(x_prompt, x_sample, rel_bias_table, w_in, w_out, attn_sink, q_norm_g, k_norm_g,
           ln1_g, ln1_b, w_gate, w_up, w_down, ln2_g, ln2_b):
    depth = w_in.shape[0]
    alpha = (2.0 * depth) ** 0.25
    within = _deinterleave_perm()
    consts = _window_tables(rel_bias_table)
    layers = []
    for l in range(depth):
        sink = (attn_sink[l].astype(F32) * LOG2E).reshape(N_KV, GROUP, 1)
        sink = jnp.broadcast_to(sink, (N_KV, GROUP, WIN)).reshape(N_KV, 1, GROUP * WIN)
        layers.append(dict(
            w_inT=_prep_w_in(w_in[l]),
            qg=q_norm_g[l].astype(F32)[within].reshape(HEAD_DIM, 1),
            kg=k_norm_g[l].astype(F32)[within].reshape(HEAD_DIM, 1),
            sink=sink,
            w_out=w_out[l].astype(BF16),
            ln1_g=ln1_g[l].astype(F32).reshape(1, -1),
            ln1_b=ln1_b[l].astype(F32).reshape(1, -1),
            w_gate=w_gate[l].astype(BF16),
            w_up=w_up[l].astype(BF16),
            w_down=w_down[l].astype(BF16),
            ln2_g=ln2_g[l].astype(F32).reshape(1, -1),
            ln2_b=ln2_b[l].astype(F32).reshape(1, -1),
        ))
    y_prompt = _trunk(x_prompt, consts, layers, alpha)
    y_sample = _trunk(x_sample, consts, layers, alpha)
    return (y_prompt, y_sample)
```

```python
import functools
import math

import jax
import jax.numpy as jnp
import numpy as np
from jax import lax
from jax.experimental import pallas as pl
from jax.experimental.pallas import tpu as pltpu

F32 = jnp.float32
BF16 = jnp.bfloat16

HEAD_DIM = 64
HALF = HEAD_DIM // 2
N_Q = 8
N_KV = 2
GROUP = N_Q // N_KV
Q_W = N_Q * HEAD_DIM
KV_W = N_KV * HEAD_DIM
WIN = 128
NUM_BUCKETS = 32
MAX_DISTANCE = 128
GRID_W = 64
ROPE_THETA = 10000.0
LN_EPS = 1e-5
RMS_EPS = 1e-6
NEG_INF = -1e30
SCALE = HEAD_DIM ** -0.5
LOG2E = math.log2(math.e)
V_ROWS = 80
CH = 512
ITEMS_PER_TRIP = 16
INPROJ_ROWS = 1024
FFN_ROWS = 1024
WIN_ITEMS_PER_TRIP = 8
VMEM_LIMIT = 56 * 1024 * 1024


def _params(n_parallel, flags=None):
    return pltpu.CompilerParams(
        dimension_semantics=("parallel",) * n_parallel,
        vmem_limit_bytes=VMEM_LIMIT, flags=flags)


def _inproj_kernel(x_ref, w_ref, cos_ref, sin_ref, qg_ref, kg_ref,
                   qa_ref, ka_ref, va_ref, qb_ref, kb_ref, vb_ref):
    tm = x_ref.shape[1]
    xb = x_ref[0].astype(BF16)

    def proj(row0, rows):
        return lax.dot_general(w_ref[row0:row0 + rows, :], xb, (((1,), (1,)), ((), ())),
                               preferred_element_type=F32)

    o_qa, o_ka = 0, Q_W
    o_qb = Q_W + 2 * KV_W
    o_kb = o_qb + Q_W

    def put_v(v_ref, vT):
        w = v_ref.shape[4]
        first_row = lax.broadcasted_iota(jnp.int32, (V_ROWS - HEAD_DIM, w), 0) == 0
        ones_pad = jnp.where(first_row, 1.0, 0.0).astype(BF16)
        for h in range(N_KV):
            for j in range(tm // w):
                v_ref[0, h, j, 0:HEAD_DIM, :] = (
                    vT[h * HEAD_DIM:(h + 1) * HEAD_DIM, j * w:(j + 1) * w].astype(BF16))
                v_ref[0, h, j, HEAD_DIM:V_ROWS, :] = ones_pad

    def put_k(k_ref, kT):
        kr = kT.T
        for h in range(N_KV):
            k_ref[0, h] = kr[:, h * HEAD_DIM:(h + 1) * HEAD_DIM].astype(BF16)

    c = cos_ref[...]
    s = sin_ref[...]

    def norm_rope(xT, g_ref, gain_scale):
        ms = jnp.mean(xT * xT, axis=0, keepdims=True)
        y = xT * lax.rsqrt(ms + RMS_EPS) * (g_ref[...] * gain_scale)
        y0 = y[:HALF]
        y1 = y[HALF:]
        return y0 * c - y1 * s, y0 * s + y1 * c

    heads_per_dot = GROUP
    for h0 in range(0, N_Q, heads_per_dot):
        qb = proj(o_qb + h0 * HEAD_DIM, heads_per_dot * HEAD_DIM)
        for hh in range(heads_per_dot):
            h = h0 + hh
            r0, r1 = norm_rope(qb[hh * HEAD_DIM:(hh + 1) * HEAD_DIM], qg_ref, SCALE * LOG2E)
            for jt in range(tm // CH):
                cols = slice(jt * CH, (jt + 1) * CH)
                qb_ref[0, jt, h * HEAD_DIM:h * HEAD_DIM + HALF, :] = r0[:, cols].astype(BF16)
                qb_ref[0, jt, h * HEAD_DIM + HALF:(h + 1) * HEAD_DIM, :] = r1[:, cols].astype(BF16)
    kvb = proj(o_kb, 2 * KV_W)
    k_rot = []
    for h in range(N_KV):
        r0, r1 = norm_rope(kvb[h * HEAD_DIM:(h + 1) * HEAD_DIM], kg_ref, 1.0)
        k_rot += [r0, r1]
    put_k(kb_ref, jnp.concatenate(k_rot, axis=0))
    put_v(vb_ref, kvb[KV_W:])

    kva = proj(o_ka, 2 * KV_W)
    put_k(ka_ref, kva[:KV_W])
    put_v(va_ref, kva[KV_W:])
    for h in range(N_KV):
        qa_s = (proj(o_qa + h * GROUP * HEAD_DIM, GROUP * HEAD_DIM)
                * (SCALE * LOG2E)).astype(BF16)
        for j in range(tm // WIN):
            for g in range(GROUP):
                qa_ref[0, j, h, :, g * WIN:(g + 1) * WIN] = (
                    qa_s[g * HEAD_DIM:(g + 1) * HEAD_DIM, j * WIN:(j + 1) * WIN])


def _inproj(x, w_inT, cosT, sinT, qg, kg):
    B, S, D = x.shape
    d_in = w_inT.shape[0]
    tm = INPROJ_ROWS
    grid = (B, S // tm)
    q_shape = jax.ShapeDtypeStruct((B, S // WIN, N_KV, HEAD_DIM, GROUP * WIN), BF16)
    k_shape = jax.ShapeDtypeStruct((B, N_KV, S, HEAD_DIM), BF16)
    va_shape = jax.ShapeDtypeStruct((B, N_KV, S // WIN, V_ROWS, WIN), BF16)
    vb_shape = jax.ShapeDtypeStruct((B, N_KV, S // CH, V_ROWS, CH), BF16)
    q_spec = pl.BlockSpec((1, tm // WIN, N_KV, HEAD_DIM, GROUP * WIN),
                          lambda b, i: (b, i, 0, 0, 0))
    qb_shape = jax.ShapeDtypeStruct((B, S // CH, Q_W, CH), BF16)
    qb_spec = pl.BlockSpec((1, tm // CH, Q_W, CH), lambda b, i: (b, i, 0, 0))
    k_spec = pl.BlockSpec((1, N_KV, tm, HEAD_DIM), lambda b, i: (b, 0, i, 0))
    va_spec = pl.BlockSpec((1, N_KV, tm // WIN, V_ROWS, WIN), lambda b, i: (b, 0, i, 0, 0))
    vb_spec = pl.BlockSpec((1, N_KV, tm // CH, V_ROWS, CH), lambda b, i: (b, 0, i, 0, 0))
    return pl.pallas_call(
        _inproj_kernel,
        grid=grid,
        in_specs=[
            pl.BlockSpec((1, tm, D), lambda b, i: (b, i, 0)),
            pl.BlockSpec((d_in, D), lambda b, i: (0, 0)),
            pl.BlockSpec((HALF, tm), lambda b, i: (0, i)),
            pl.BlockSpec((HALF, tm), lambda b, i: (0, i)),
            pl.BlockSpec((HEAD_DIM, 1), lambda b, i: (0, 0)),
            pl.BlockSpec((HEAD_DIM, 1), lambda b, i: (0, 0)),
        ],
        out_specs=[q_spec, k_spec, va_spec, qb_spec, k_spec, vb_spec],
        out_shape=[q_shape, k_shape, va_shape, qb_shape, k_shape, vb_shape],
        compiler_params=_params(2),
        name="inproj",
    )(x, w_inT, cosT, sinT, qg, kg)


def _global_attn_kernel(qT_ref, k_ref, vT_ref, o_ref, s_scr, p_scr, cmax_scr, alpha_scr, acc_scr):
    nq, _, tq = qT_ref.shape[1:]
    n = vT_ref.shape[2]
    per_tile = GROUP * n
    n_items = nq * per_tile
    R = s_scr.shape[0]
    assert n % R == 0 or R % n == 0
    trips_per_head = max(n // R, 1)
    slots_per_head = min(n, R)

    def decode(item):
        rem = item % per_tile
        return item // per_tile, rem // n, rem % n

    def head_rows(g):
        return pl.ds(pl.multiple_of(g * HEAD_DIM, HEAD_DIM), HEAD_DIM)

    def scores(slot, item):
        qt, g, c = decode(item)
        kc = k_ref[0, 0, pl.ds(pl.multiple_of(c * CH, CH), CH), :]
        s = jnp.dot(kc, qT_ref[0, qt, head_rows(g), :], preferred_element_type=F32)
        s_scr[slot] = s
        cmax_scr[slot] = jnp.max(s, axis=0, keepdims=True)

    def probs(slot, m_prev):
        m = jnp.maximum(m_prev, cmax_scr[slot])
        alpha_scr[slot] = jnp.exp2(m_prev - m)
        p_scr[slot] = jnp.exp2(s_scr[slot] - m).astype(BF16)
        return m

    def accumulate(slot, item, acc):
        _, _, c = decode(jnp.maximum(item, 0))
        return acc * alpha_scr[slot] + jnp.dot(vT_ref[0, 0, c], p_scr[slot],
                                               preferred_element_type=F32)

    def emit(slot, item, acc):
        if (slot + 1) % slots_per_head == 0:
            qt, g, _ = decode(jnp.maximum(item, 0))
            o_ref[0, qt, head_rows(g), :] = (
                acc[:HEAD_DIM] * (1.0 / acc[HEAD_DIM:HEAD_DIM + 1])).astype(BF16)

    def trip(u, m, with_scores):
        t0 = R * u
        acc = acc_scr[...]
        for j in range(R):
            if j % slots_per_head == 0:
                m = jnp.where(u % trips_per_head == 0, NEG_INF, m)
            acc = accumulate(j, t0 - R + j, acc)
            emit(j, t0 - R + j, acc)
            m = probs(j, m)
            if with_scores:
                scores(j, t0 + R + j)
        acc_scr[...] = acc
        return m

    n_trips = n_items // R
    for j in range(R):
        scores(j, j)
    p_scr[...] = jnp.zeros(p_scr.shape, BF16)
    alpha_scr[...] = jnp.ones(alpha_scr.shape, F32)
    acc_scr[...] = jnp.ones(acc_scr.shape, F32)
    m = lax.fori_loop(0, n_trips - 1, lambda u, m: trip(u, m, True),
                      jnp.full((1, tq), NEG_INF, F32))
    trip(n_trips - 1, m, False)
    acc = acc_scr[...]
    for j in range(R):
        acc = accumulate(j, n_items - R + j, acc)
        emit(j, n_items - R + j, acc)


def _global_attn(qT, k, vT):
    B, nq, _, tq = qT.shape
    S = k.shape[2]
    gw = GROUP * HEAD_DIM
    R = ITEMS_PER_TRIP
    return pl.pallas_call(
        _global_attn_kernel,
        grid=(B, N_KV),
        in_specs=[
            pl.BlockSpec((1, nq, gw, tq), lambda b, h: (b, 0, h, 0)),
            pl.BlockSpec((1, 1, S, HEAD_DIM), lambda b, h: (b, h, 0, 0)),
            pl.BlockSpec((1, 1, S // CH, V_ROWS, CH), lambda b, h: (b, h, 0, 0, 0)),
        ],
        out_specs=pl.BlockSpec((1, nq, gw, tq), lambda b, h: (b, 0, h, 0)),
        out_shape=jax.ShapeDtypeStruct((B, nq, Q_W, tq), BF16),
        scratch_shapes=[pltpu.VMEM((R, CH, tq), F32),
                        pltpu.VMEM((R, CH, tq), BF16),
                        pltpu.VMEM((R, 1, tq), F32),
                        pltpu.VMEM((R, 1, tq), F32),
                        pltpu.VMEM((V_ROWS, tq), F32)],
        compiler_params=_params(2),
        name="global_attn",
    )(qT, k, vT)


def _window_attn_kernel(q_ref, k_ref, vT_ref, bias_ref, mask_ref, sink_ref, o_ref,
                        s_scr, p_scr, m_scr, sinkp_scr):
    nb = q_ref.shape[1]
    R = WIN_ITEMS_PER_TRIP
    assert nb % R == 0
    n_trips = nb // R
    sink = sink_ref[0]

    def key_block(i, c):
        kb = i - 1 + c
        return kb, jnp.clip(kb, 0, nb - 1)

    def scores(slot, i):
        q = q_ref[0, i, 0]
        m = sink
        for c in range(3):
            kb, kb_c = key_block(i, c)
            kc = k_ref[0, 0, pl.ds(pl.multiple_of(kb_c * WIN, WIN), WIN), :]
            s = jnp.dot(kc, q, preferred_element_type=F32) + bias_ref[0, c]
            if c != 1:
                thr = jnp.where(kb == kb_c, 0.5, 2.0)
                s = jnp.where(mask_ref[c] > thr, s, NEG_INF)
            s_scr[slot, c] = s
            m = jnp.maximum(m, jnp.max(s, axis=0, keepdims=True))
        m_scr[slot] = m

    def probs(slot):
        m = m_scr[slot]
        sinkp_scr[slot] = jnp.exp2(sink - m)
        for c in range(3):
            p_scr[slot, c] = jnp.exp2(s_scr[slot, c] - m).astype(BF16)

    def output(slot, i):
        i = jnp.maximum(i, 0)
        acc = None
        for c in range(3):
            _, kb_c = key_block(i, c)
            pv = jnp.dot(vT_ref[0, 0, kb_c], p_scr[slot, c], preferred_element_type=F32)
            acc = pv if acc is None else acc + pv
        denom = acc[HEAD_DIM:HEAD_DIM + 1] + sinkp_scr[slot]
        oT = (acc[:HEAD_DIM] * (1.0 / denom)).astype(BF16)
        for g in range(GROUP):
            o_ref[0, i, g * HEAD_DIM:(g + 1) * HEAD_DIM, :] = oT[:, g * WIN:(g + 1) * WIN]

    def trip(u, with_scores):
        t0 = R * u
        for j in range(R):
            output(j, t0 - R + j)
            probs(j)
            if with_scores:
                scores(j, t0 + R + j)

    for j in range(R):
        scores(j, j)
    p_scr[...] = jnp.zeros(p_scr.shape, BF16)
    sinkp_scr[...] = jnp.ones(sinkp_scr.shape, F32)

    def body(u, carry):
        trip(u, True)
        return carry

    lax.fori_loop(0, n_trips - 1, body, 0)
    trip(n_trips - 1, False)
    for j in range(R):
        output(j, nb - R + j)


def _window_attn(q, k, vT, biasT, maskT, sink_row):
    B, nb = q.shape[:2]
    S = k.shape[2]
    gw = GROUP * HEAD_DIM
    gl = GROUP * WIN
    R = WIN_ITEMS_PER_TRIP
    return pl.pallas_call(
        _window_attn_kernel,
        grid=(B, N_KV),
        in_specs=[
            pl.BlockSpec((1, nb, 1, HEAD_DIM, gl), lambda b, h: (b, 0, h, 0, 0)),
            pl.BlockSpec((1, 1, S, HEAD_DIM), lambda b, h: (b, h, 0, 0)),
            pl.BlockSpec((1, 1, nb, V_ROWS, WIN), lambda b, h: (b, h, 0, 0, 0)),
            pl.BlockSpec((1, 3, WIN, gl), lambda b, h: (h, 0, 0, 0)),
            pl.BlockSpec((3, WIN, gl), lambda b, h: (0, 0, 0)),
            pl.BlockSpec((1, 1, gl), lambda b, h: (h, 0, 0)),
        ],
        out_specs=pl.BlockSpec((1, nb, gw, WIN), lambda b, h: (b, 0, h, 0)),
        out_shape=jax.ShapeDtypeStruct((B, nb, Q_W, WIN), BF16),
        scratch_shapes=[pltpu.VMEM((R, 3, WIN, gl), F32),
                        pltpu.VMEM((R, 3, WIN, gl), BF16),
                        pltpu.VMEM((R, 1, gl), F32),
                        pltpu.VMEM((R, 1, gl), F32)],
        compiler_params=_params(2),
        name="window_attn",
    )(q, k, vT, biasT, maskT, sink_row)


def _layer_norm(v, g, b):
    mu = jnp.mean(v, axis=-1, keepdims=True)
    d = v - mu
    var = jnp.mean(d * d, axis=-1, keepdims=True)
    return d * lax.rsqrt(var + LN_EPS) * g + b


def _mix_ffn_kernel(oa_ref, ob_ref, x_ref, wo_ref, g1_ref, b1_ref, wg_ref, wu_ref, wd_ref,
                    g2_ref, b2_ref, y_ref, *, alpha, ff_chunk):
    oaT = jnp.concatenate([oa_ref[0, j] for j in range(oa_ref.shape[1])], axis=1)
    obT = jnp.concatenate([ob_ref[0, j] for j in range(ob_ref.shape[1])], axis=1)
    oT = jnp.concatenate([oaT, obT], axis=0)
    mixed = lax.dot_general(oT, wo_ref[...], (((0,), (0,)), ((), ())),
                            preferred_element_type=F32)
    x1 = _layer_norm(alpha * x_ref[0] + mixed, g1_ref[...], b1_ref[...])
    xb = x1.astype(BF16)
    d_ff = wg_ref.shape[1]
    acc = None
    for c0 in range(0, d_ff, ff_chunk):
        gate = jnp.dot(xb, wg_ref[:, c0:c0 + ff_chunk], preferred_element_type=F32)
        up = jnp.dot(xb, wu_ref[:, c0:c0 + ff_chunk], preferred_element_type=F32)
        h = (gate * jax.nn.sigmoid(gate) * up).astype(BF16)
        part = jnp.dot(h, wd_ref[c0:c0 + ff_chunk, :], preferred_element_type=F32)
        acc = part if acc is None else acc + part
    y_ref[0] = _layer_norm(alpha * x1 + acc, g2_ref[...], b2_ref[...])


def _mix_ffn(oa, ob, x, lw, alpha):
    B, S, D = x.shape
    d_ff = lw["w_gate"].shape[1]
    tm = FFN_ROWS
    ff_chunk = 256
    assert d_ff % ff_chunk == 0
    row = lambda bb, i: (bb, i, 0)
    const = lambda bb, i: (0, 0)
    once = pl.Buffered(1)
    vec = pl.BlockSpec((1, D), const)
    return pl.pallas_call(
        functools.partial(_mix_ffn_kernel, alpha=alpha, ff_chunk=ff_chunk),
        grid=(B, S // tm),
        in_specs=[
            pl.BlockSpec((1, tm // WIN, Q_W, WIN), lambda bb, i: (bb, i, 0, 0)),
            pl.BlockSpec((1, tm // CH, Q_W, CH), lambda bb, i: (bb, i, 0, 0)),
            pl.BlockSpec((1, tm, D), row),
            pl.BlockSpec((2 * Q_W, D), const, pipeline_mode=once),
            vec, vec,
            pl.BlockSpec((D, d_ff), const, pipeline_mode=once),
            pl.BlockSpec((D, d_ff), const, pipeline_mode=once),
            pl.BlockSpec((d_ff, D), const, pipeline_mode=once),
            vec, vec,
        ],
        out_specs=pl.BlockSpec((1, tm, D), row),
        out_shape=jax.ShapeDtypeStruct((B, S, D), F32),
        compiler_params=_params(2),
        name="mix_ffn",
    )(oa, ob, x, lw["w_out"], lw["ln1_g"], lw["ln1_b"], lw["w_gate"], lw["w_up"], lw["w_down"],
      lw["ln2_g"], lw["ln2_b"])


def _t5_bucket(rel):
    nb = NUM_BUCKETS // 2
    max_exact = nb // 2
    bucket = (rel > 0).astype(jnp.int32) * nb
    n = jnp.abs(rel)
    n_f = jnp.maximum(n, 1).astype(F32)
    large = max_exact + (jnp.log(n_f / max_exact) / math.log(MAX_DISTANCE / max_exact)
                         * (nb - max_exact)).astype(jnp.int32)
    large = jnp.minimum(large, nb - 1)
    return bucket + jnp.where(n < max_exact, n, large)


def _window_tables(rel_table):
    key = jnp.arange(3 * WIN)[:, None]
    qry = jnp.arange(WIN)[None, :]
    rel = (key - WIN) - qry
    onehot = (_t5_bucket(rel)[..., None] == jnp.arange(NUM_BUCKETS)).astype(F32)
    bias = jnp.einsum("kqb,bh->kqh", onehot, rel_table.astype(F32),
                      precision=lax.Precision.HIGHEST)
    bias = bias.reshape(3, WIN, WIN, N_KV, GROUP).transpose(3, 0, 1, 4, 2)
    bias = bias.reshape(N_KV, 3, WIN, GROUP * WIN) * LOG2E
    mask = (jnp.abs(rel) <= WIN).astype(F32).reshape(3, WIN, 1, WIN)
    mask = jnp.broadcast_to(mask, (3, WIN, GROUP, WIN)).reshape(3, WIN, GROUP * WIN)
    return bias, mask


def _rope_tables(S):
    rows = S // GRID_W
    row = jnp.repeat(jnp.arange(rows), GRID_W).astype(F32)
    col = jnp.tile(jnp.arange(GRID_W), rows).astype(F32)
    axis_dim = HEAD_DIM // 2
    inv = ROPE_THETA ** (-jnp.arange(0, axis_dim, 2, dtype=F32) / axis_dim)
    ang = jnp.concatenate([row[:, None] * inv, col[:, None] * inv], axis=-1)
    return jnp.cos(ang).T, jnp.sin(ang).T


def _deinterleave_perm():
    within = np.concatenate([np.arange(0, HEAD_DIM, 2), np.arange(1, HEAD_DIM, 2)])
    return within


def _prep_w_in(w_in_l):
    within = _deinterleave_perm()
    cols = np.arange(w_in_l.shape[1])
    base_qb = Q_W + 2 * KV_W
    for h in range(N_Q + N_KV):
        lo = base_qb + h * HEAD_DIM
        cols[lo:lo + HEAD_DIM] = lo + within
    return w_in_l[:, cols].T.astype(BF16)


def _trunk(x, consts, layers, alpha):
    S = x.shape[1]
    cosT, sinT = _rope_tables(S)
    biasT, maskT = consts
    for lw in layers:
        qa, ka, va, qb, kb, vb = _inproj(x, lw["w_inT"], cosT, sinT, lw["qg"], lw["kg"])
        oa = _window_attn(qa, ka, va, biasT, maskT, lw["sink"])
        ob = _global_attn(qb, kb, vb)
        x = _mix_ffn(oa, ob, x, lw, alpha)
    return x


def kernel(x_prompt, x_sample, rel_bias_table, w_in, w_out, attn_sink, q_norm_g, k_norm_g,
           ln1_g, ln1_b, w_gate, w_up, w_down, ln2_g, ln2_b):
    depth = w_in.shape[0]
    alpha = (2.0 * depth) ** 0.25
    within = _deinterleave_perm()
    consts = _window_tables(rel_bias_table)
    layers = []
    for l in range(depth):
        sink = (attn_sink[l].astype(F32) * LOG2E).reshape(N_KV, GROUP, 1)
        sink = jnp.broadcast_to(sink, (N_KV, GROUP, WIN)).reshape(N_KV, 1, GROUP * WIN)
        layers.append(dict(
            w_inT=_prep_w_in(w_in[l]),
            qg=q_norm_g[l].astype(F32)[within].reshape(HEAD_DIM, 1),
            kg=k_norm_g[l].astype(F32)[within].reshape(HEAD_DIM, 1),
            sink=sink,
            w_out=w_out[l].astype(BF16),
            ln1_g=ln1_g[l].astype(F32).reshape(1, -1),
            ln1_b=ln1_b[l].astype(F32).reshape(1, -1),
            w_gate=w_gate[l].astype(BF16),
            w_up=w_up[l].astype(BF16),
            w_down=w_down[l].astype(BF16),
            ln2_g=ln2_g[l].astype(F32).reshape(1, -1),
            ln2_b=ln2_b[l].astype(F32).reshape(1, -1),
        ))
    y_prompt = _trunk(x_prompt, consts, layers, alpha)
    y_sample = _trunk(x_sample, consts, layers, alpha)
    return (y_prompt, y_sample)
```

```python
import functools
import math

import jax
import jax.numpy as jnp
import numpy as np
from jax import lax
from jax.experimental import pallas as pl
from jax.experimental.pallas import tpu as pltpu

F32 = jnp.float32
BF16 = jnp.bfloat16

HEAD_DIM = 64
HALF = HEAD_DIM // 2
N_Q = 8
N_KV = 2
GROUP = N_Q // N_KV
Q_W = N_Q * HEAD_DIM
KV_W = N_KV * HEAD_DIM
WIN = 128
NUM_BUCKETS = 32
MAX_DISTANCE = 128
GRID_W = 64
ROPE_THETA = 10000.0
LN_EPS = 1e-5
RMS_EPS = 1e-6
NEG_INF = -1e30
SCALE = HEAD_DIM ** -0.5
LOG2E = math.log2(math.e)
V_ROWS = 80
CH = 512
ITEMS_PER_TRIP = 16
INPROJ_ROWS = 1024
FFN_ROWS = 1024
WIN_ITEMS_PER_TRIP = 8
VMEM_LIMIT = 56 * 1024 * 1024


def _params(n_parallel, flags=None):
    return pltpu.CompilerParams(
        dimension_semantics=("parallel",) * n_parallel,
        vmem_limit_bytes=VMEM_LIMIT, flags=flags)


def _inproj_kernel(x_ref, w_ref, cos_ref, sin_ref, qg_ref, kg_ref,
                   qa_ref, ka_ref, va_ref, qb_ref, kb_ref, vb_ref):
    tm = x_ref.shape[1]
    xb = x_ref[0].astype(BF16)

    def proj(row0, rows):
        return lax.dot_general(w_ref[row0:row0 + rows, :], xb, (((1,), (1,)), ((), ())),
                               preferred_element_type=F32)

    o_qa, o_ka = 0, Q_W
    o_qb = Q_W + 2 * KV_W
    o_kb = o_qb + Q_W

    def put_v(v_ref, vT):
        w = v_ref.shape[4]
        first_row = lax.broadcasted_iota(jnp.int32, (V_ROWS - HEAD_DIM, w), 0) == 0
        ones_pad = jnp.where(first_row, 1.0, 0.0).astype(BF16)
        for h in range(N_KV):
            for j in range(tm // w):
                v_ref[0, h, j, 0:HEAD_DIM, :] = (
                    vT[h * HEAD_DIM:(h + 1) * HEAD_DIM, j * w:(j + 1) * w].astype(BF16))
                v_ref[0, h, j, HEAD_DIM:V_ROWS, :] = ones_pad

    def put_k(k_ref, kT):
        kr = kT.T
        for h in range(N_KV):
            k_ref[0, h] = kr[:, h * HEAD_DIM:(h + 1) * HEAD_DIM].astype(BF16)

    c = cos_ref[...]
    s = sin_ref[...]

    def norm_rope(xT, g_ref, gain_scale):
        ms = jnp.mean(xT * xT, axis=0, keepdims=True)
        y = xT * lax.rsqrt(ms + RMS_EPS) * (g_ref[...] * gain_scale)
        y0 = y[:HALF]
        y1 = y[HALF:]
        return y0 * c - y1 * s, y0 * s + y1 * c

    heads_per_dot = GROUP
    for h0 in range(0, N_Q, heads_per_dot):
        qb = proj(o_qb + h0 * HEAD_DIM, heads_per_dot * HEAD_DIM)
        for hh in range(heads_per_dot):
            h = h0 + hh
            r0, r1 = norm_rope(qb[hh * HEAD_DIM:(hh + 1) * HEAD_DIM], qg_ref, SCALE * LOG2E)
            for jt in range(tm // CH):
                cols = slice(jt * CH, (jt + 1) * CH)
                qb_ref[0, jt, h * HEAD_DIM:h * HEAD_DIM + HALF, :] = r0[:, cols].astype(BF16)
                qb_ref[0, jt, h * HEAD_DIM + HALF:(h + 1) * HEAD_DIM, :] = r1[:, cols].astype(BF16)
    kvb = proj(o_kb, 2 * KV_W)
    k_rot = []
    for h in range(N_KV):
        r0, r1 = norm_rope(kvb[h * HEAD_DIM:(h + 1) * HEAD_DIM], kg_ref, 1.0)
        k_rot += [r0, r1]
    put_k(kb_ref, jnp.concatenate(k_rot, axis=0))
    put_v(vb_ref, kvb[KV_W:])

    kva = proj(o_ka, 2 * KV_W)
    put_k(ka_ref, kva[:KV_W])
    put_v(va_ref, kva[KV_W:])
    for h in range(N_KV):
        qa_s = (proj(o_qa + h * GROUP * HEAD_DIM, GROUP * HEAD_DIM)
                * (SCALE * LOG2E)).astype(BF16)
        for j in range(tm // WIN):
            for g in range(GROUP):
                qa_ref[0, j, h, :, g * WIN:(g + 1) * WIN] = (
                    qa_s[g * HEAD_DIM:(g + 1) * HEAD_DIM, j * WIN:(j + 1) * WIN])


def _inproj(x, w_inT, cosT, sinT, qg, kg):
    B, S, D = x.shape
    d_in = w_inT.shape[0]
    tm = INPROJ_ROWS
    grid = (B, S // tm)
    q_shape = jax.ShapeDtypeStruct((B, S // WIN, N_KV, HEAD_DIM, GROUP * WIN), BF16)
    k_shape = jax.ShapeDtypeStruct((B, N_KV, S, HEAD_DIM), BF16)
    va_shape = jax.ShapeDtypeStruct((B, N_KV, S // WIN, V_ROWS, WIN), BF16)
    vb_shape = jax.ShapeDtypeStruct((B, N_KV, S // CH, V_ROWS, CH), BF16)
    q_spec = pl.BlockSpec((1, tm // WIN, N_KV, HEAD_DIM, GROUP * WIN),
                          lambda b, i: (b, i, 0, 0, 0))
    qb_shape = jax.ShapeDtypeStruct((B, S // CH, Q_W, CH), BF16)
    qb_spec = pl.BlockSpec((1, tm // CH, Q_W, CH), lambda b, i: (b, i, 0, 0))
    k_spec = pl.BlockSpec((1, N_KV, tm, HEAD_DIM), lambda b, i: (b, 0, i, 0))
    va_spec = pl.BlockSpec((1, N_KV, tm // WIN, V_ROWS, WIN), lambda b, i: (b, 0, i, 0, 0))
    vb_spec = pl.BlockSpec((1, N_KV, tm // CH, V_ROWS, CH), lambda b, i: (b, 0, i, 0, 0))
    return pl.pallas_call(
        _inproj_kernel,
        grid=grid,
        in_specs=[
            pl.BlockSpec((1, tm, D), lambda b, i: (b, i, 0)),
            pl.BlockSpec((d_in, D), lambda b, i: (0, 0)),
            pl.BlockSpec((HALF, tm), lambda b, i: (0, i)),
            pl.BlockSpec((HALF, tm), lambda b, i: (0, i)),
            pl.BlockSpec((HEAD_DIM, 1), lambda b, i: (0, 0)),
            pl.BlockSpec((HEAD_DIM, 1), lambda b, i: (0, 0)),
        ],
        out_specs=[q_spec, k_spec, va_spec, qb_spec, k_spec, vb_spec],
        out_shape=[q_shape, k_shape, va_shape, qb_shape, k_shape, vb_shape],
        compiler_params=_params(2),
        name="inproj",
    )(x, w_inT, cosT, sinT, qg, kg)


def _global_attn_kernel(qT_ref, k_ref, vT_ref, o_ref, s_scr, p_scr, cmax_scr, alpha_scr, acc_scr):
    nq, _, tq = qT_ref.shape[1:]
    n = vT_ref.shape[2]
    per_tile = GROUP * n
    n_items = nq * per_tile
    R = s_scr.shape[0]
    assert n % R == 0 or R % n == 0
    trips_per_head = max(n // R, 1)
    slots_per_head = min(n, R)

    def decode(item):
        rem = item % per_tile
        return item // per_tile, rem // n, rem % n

    def head_rows(g):
        return pl.ds(pl.multiple_of(g * HEAD_DIM, HEAD_DIM), HEAD_DIM)

    def scores(slot, item):
        qt, g, c = decode(item)
        kc = k_ref[0, 0, pl.ds(pl.multiple_of(c * CH, CH), CH), :]
        s = jnp.dot(kc, qT_ref[0, qt, head_rows(g), :], preferred_element_type=F32)
        s_scr[slot] = s
        cmax_scr[slot] = jnp.max(s, axis=0, keepdims=True)

    def probs(slot, m_prev):
        m = jnp.maximum(m_prev, cmax_scr[slot])
        alpha_scr[slot] = jnp.exp2(m_prev - m)
        p_scr[slot] = jnp.exp2(s_scr[slot] - m).astype(BF16)
        return m

    def accumulate(slot, item, acc):
        _, _, c = decode(jnp.maximum(item, 0))
        return acc * alpha_scr[slot] + jnp.dot(vT_ref[0, 0, c], p_scr[slot],
                                               preferred_element_type=F32)

    def emit(slot, item, acc):
        if (slot + 1) % slots_per_head == 0:
            qt, g, _ = decode(jnp.maximum(item, 0))
            o_ref[0, qt, head_rows(g), :] = (
                acc[:HEAD_DIM] * (1.0 / acc[HEAD_DIM:HEAD_DIM + 1])).astype(BF16)

    def trip(u, m, with_scores):
        t0 = R * u
        acc = acc_scr[...]
        for j in range(R):
            if j % slots_per_head == 0:
                m = jnp.where(u % trips_per_head == 0, NEG_INF, m)
            acc = accumulate(j, t0 - R + j, acc)
            emit(j, t0 - R + j, acc)
            m = probs(j, m)
            if with_scores:
                scores(j, t0 + R + j)
        acc_scr[...] = acc
        return m

    n_trips = n_items // R
    for j in range(R):
        scores(j, j)
    p_scr[...] = jnp.zeros(p_scr.shape, BF16)
    alpha_scr[...] = jnp.ones(alpha_scr.shape, F32)
    acc_scr[...] = jnp.ones(acc_scr.shape, F32)
    m = lax.fori_loop(0, n_trips - 1, lambda u, m: trip(u, m, True),
                      jnp.full((1, tq), NEG_INF, F32))
    trip(n_trips - 1, m, False)
    acc = acc_scr[...]
    for j in range(R):
        acc = accumulate(j, n_items - R + j, acc)
        emit(j, n_items - R + j, acc)


def _global_attn(qT, k, vT):
    B, nq, _, tq = qT.shape
    S = k.shape[2]
    gw = GROUP * HEAD_DIM
    R = ITEMS_PER_TRIP
    return pl.pallas_call(
        _global_attn_kernel,
        grid=(B, N_KV),
        in_specs=[
            pl.BlockSpec((1, nq, gw, tq), lambda b, h: (b, 0, h, 0)),
            pl.BlockSpec((1, 1, S, HEAD_DIM), lambda b, h: (b, h, 0, 0)),
            pl.BlockSpec((1, 1, S // CH, V_ROWS, CH), lambda b, h: (b, h, 0, 0, 0)),
        ],
        out_specs=pl.BlockSpec((1, nq, gw, tq), lambda b, h: (b, 0, h, 0)),
        out_shape=jax.ShapeDtypeStruct((B, nq, Q_W, tq), BF16),
        scratch_shapes=[pltpu.VMEM((R, CH, tq), F32),
                        pltpu.VMEM((R, CH, tq), BF16),
                        pltpu.VMEM((R, 1, tq), F32),
                        pltpu.VMEM((R, 1, tq), F32),
                        pltpu.VMEM((V_ROWS, tq), F32)],
        compiler_params=_params(2),
        name="global_attn",
    )(qT, k, vT)


def _window_attn_kernel(q_ref, k_ref, vT_ref, bias_ref, mask_ref, sink_ref, o_ref,
                        s_scr, p_scr, m_scr, sinkp_scr):
    nb = q_ref.shape[1]
    R = WIN_ITEMS_PER_TRIP
    assert nb % R == 0
    n_trips = nb // R
    sink = sink_ref[0]

    def key_block(i, c):
        kb = i - 1 + c
        return kb, jnp.clip(kb, 0, nb - 1)

    def scores(slot, i):
        q = q_ref[0, i, 0]
        m = sink
        for c in range(3):
            kb, kb_c = key_block(i, c)
            kc = k_ref[0, 0, pl.ds(pl.multiple_of(kb_c * WIN, WIN), WIN), :]
            s = jnp.dot(kc, q, preferred_element_type=F32) + bias_ref[0, c]
            if c != 1:
                thr = jnp.where(kb == kb_c, 0.5, 2.0)
                s = jnp.where(mask_ref[c] > thr, s, NEG_INF)
            s_scr[slot, c] = s
            m = jnp.maximum(m, jnp.max(s, axis=0, keepdims=True))
        m_scr[slot] = m

    def probs(slot):
        m = m_scr[slot]
        sinkp_scr[slot] = jnp.exp2(sink - m)
        for c in range(3):
            p_scr[slot, c] = jnp.exp2(s_scr[slot, c] - m).astype(BF16)

    def output(slot, i):
        i = jnp.maximum(i, 0)
        acc = None
        for c in range(3):
            _, kb_c = key_block(i, c)
            pv = jnp.dot(vT_ref[0, 0, kb_c], p_scr[slot, c], preferred_element_type=F32)
            acc = pv if acc is None else acc + pv
        denom = acc[HEAD_DIM:HEAD_DIM + 1] + sinkp_scr[slot]
        oT = (acc[:HEAD_DIM] * (1.0 / denom)).astype(BF16)
        for g in range(GROUP):
            o_ref[0, i, g * HEAD_DIM:(g + 1) * HEAD_DIM, :] = oT[:, g * WIN:(g + 1) * WIN]

    def trip(u, with_scores):
        t0 = R * u
        for j in range(R):
            output(j, t0 - R + j)
            probs(j)
            if with_scores:
                scores(j, t0 + R + j)

    for j in range(R):
        scores(j, j)
    p_scr[...] = jnp.zeros(p_scr.shape, BF16)
    sinkp_scr[...] = jnp.ones(sinkp_scr.shape, F32)

    def body(u, carry):
        trip(u, True)
        return carry

    lax.fori_loop(0, n_trips - 1, body, 0)
    trip(n_trips - 1, False)
    for j in range(R):
        output(j, nb - R + j)


def _window_attn(q, k, vT, biasT, maskT, sink_row):
    B, nb = q.shape[:2]
    S = k.shape[2]
    gw = GROUP * HEAD_DIM
    gl = GROUP * WIN
    R = WIN_ITEMS_PER_TRIP
    return pl.pallas_call(
        _window_attn_kernel,
        grid=(B, N_KV),
        in_specs=[
            pl.BlockSpec((1, nb, 1, HEAD_DIM, gl), lambda b, h: (b, 0, h, 0, 0)),
            pl.BlockSpec((1, 1, S, HEAD_DIM), lambda b, h: (b, h, 0, 0)),
            pl.BlockSpec((1, 1, nb, V_ROWS, WIN), lambda b, h: (b, h, 0, 0, 0)),
            pl.BlockSpec((1, 3, WIN, gl), lambda b, h: (h, 0, 0, 0)),
            pl.BlockSpec((3, WIN, gl), lambda b, h: (0, 0, 0)),
            pl.BlockSpec((1, 1, gl), lambda b, h: (h, 0, 0)),
        ],
        out_specs=pl.BlockSpec((1, nb, gw, WIN), lambda b, h: (b, 0, h, 0)),
        out_shape=jax.ShapeDtypeStruct((B, nb, Q_W, WIN), BF16),
        scratch_shapes=[pltpu.VMEM((R, 3, WIN, gl), F32),
                        pltpu.VMEM((R, 3, WIN, gl), BF16),
                        pltpu.VMEM((R, 1, gl), F32),
                        pltpu.VMEM((R, 1, gl), F32)],
        compiler_params=_params(2),
        name="window_attn",
    )(q, k, vT, biasT, maskT, sink_row)


def _layer_norm(v, g, b):
    mu = jnp.mean(v, axis=-1, keepdims=True)
    d = v - mu
    var = jnp.mean(d * d, axis=-1, keepdims=True)
    return d * lax.rsqrt(var + LN_EPS) * g + b


def _mix_ffn_kernel(oa_ref, ob_ref, x_ref, wo_ref, g1_ref, b1_ref, wg_ref, wu_ref, wd_ref,
                    g2_ref, b2_ref, y_ref, *, alpha, ff_chunk):
    d_ff = wg_ref.shape[1]
    wins_per_group = CH // WIN

    def mixed_norm(r):
        oaT = jnp.concatenate([oa_ref[0, r * wins_per_group + j] for j in range(wins_per_group)],
                              axis=1)
        oT = jnp.concatenate([oaT, ob_ref[0, r]], axis=0)
        mixed = lax.dot_general(oT, wo_ref[...], (((0,), (0,)), ((), ())),
                                preferred_element_type=F32)
        return _layer_norm(alpha * x_ref[0, r * CH:(r + 1) * CH, :] + mixed,
                           g1_ref[...], b1_ref[...])

    def zero_after(v):
        return jnp.sum(v, axis=0, keepdims=True) * 0.0

    def ffn(x1, first_after, last_after):
        xb = x1.astype(BF16)
        n_chunks = d_ff // ff_chunk
        acc = None
        for ci in range(n_chunks):
            c0 = ci * ff_chunk
            lhs = xb
            if ci == 0 and first_after is not None:
                lhs = (x1 + first_after).astype(BF16)
            if ci == n_chunks - 1 and last_after is not None:
                lhs = (x1 + last_after).astype(BF16)
            gate = jnp.dot(lhs, wg_ref[:, c0:c0 + ff_chunk], preferred_element_type=F32)
            up = jnp.dot(lhs, wu_ref[:, c0:c0 + ff_chunk], preferred_element_type=F32)
            h = (gate * jax.nn.sigmoid(gate) * up).astype(BF16)
            part = jnp.dot(h, wd_ref[c0:c0 + ff_chunk, :], preferred_element_type=F32)
            acc = part if acc is None else acc + part
        return acc

    x1a = mixed_norm(0)
    x1b = mixed_norm(1)
    acc_a = ffn(x1a, None, zero_after(x1b))
    y_ref[0, 0:CH, :] = _layer_norm(alpha * x1a + acc_a, g2_ref[...], b2_ref[...])
    acc_b = ffn(x1b, zero_after(acc_a), None)
    y_ref[0, CH:2 * CH, :] = _layer_norm(alpha * x1b + acc_b, g2_ref[...], b2_ref[...])


def _mix_ffn(oa, ob, x, lw, alpha):
    B, S, D = x.shape
    d_ff = lw["w_gate"].shape[1]
    tm = FFN_ROWS
    ff_chunk = 256
    assert d_ff % ff_chunk == 0
    row = lambda bb, i: (bb, i, 0)
    const = lambda bb, i: (0, 0)
    once = pl.Buffered(1)
    vec = pl.BlockSpec((1, D), const)
    return pl.pallas_call(
        functools.partial(_mix_ffn_kernel, alpha=alpha, ff_chunk=ff_chunk),
        grid=(B, S // tm),
        in_specs=[
            pl.BlockSpec((1, tm // WIN, Q_W, WIN), lambda bb, i: (bb, i, 0, 0)),
            pl.BlockSpec((1, tm // CH, Q_W, CH), lambda bb, i: (bb, i, 0, 0)),
            pl.BlockSpec((1, tm, D), row),
            pl.BlockSpec((2 * Q_W, D), const, pipeline_mode=once),
            vec, vec,
            pl.BlockSpec((D, d_ff), const, pipeline_mode=once),
            pl.BlockSpec((D, d_ff), const, pipeline_mode=once),
            pl.BlockSpec((d_ff, D), const, pipeline_mode=once),
            vec, vec,
        ],
        out_specs=pl.BlockSpec((1, tm, D), row),
        out_shape=jax.ShapeDtypeStruct((B, S, D), F32),
        compiler_params=_params(2),
        name="mix_ffn",
    )(oa, ob, x, lw["w_out"], lw["ln1_g"], lw["ln1_b"], lw["w_gate"], lw["w_up"], lw["w_down"],
      lw["ln2_g"], lw["ln2_b"])


def _t5_bucket(rel):
    nb = NUM_BUCKETS // 2
    max_exact = nb // 2
    bucket = (rel > 0).astype(jnp.int32) * nb
    n = jnp.abs(rel)
    n_f = jnp.maximum(n, 1).astype(F32)
    large = max_exact + (jnp.log(n_f / max_exact) / math.log(MAX_DISTANCE / max_exact)
                         * (nb - max_exact)).astype(jnp.int32)
    large = jnp.minimum(large, nb - 1)
    return bucket + jnp.where(n < max_exact, n, large)


def _window_tables(rel_table):
    key = jnp.arange(3 * WIN)[:, None]
    qry = jnp.arange(WIN)[None, :]
    rel = (key - WIN) - qry
    onehot = (_t5_bucket(rel)[..., None] == jnp.arange(NUM_BUCKETS)).astype(F32)
    bias = jnp.einsum("kqb,bh->kqh", onehot, rel_table.astype(F32),
                      precision=lax.Precision.HIGHEST)
    bias = bias.reshape(3, WIN, WIN, N_KV, GROUP).transpose(3, 0, 1, 4, 2)
    bias = bias.reshape(N_KV, 3, WIN, GROUP * WIN) * LOG2E
    mask = (jnp.abs(rel) <= WIN).astype(F32).reshape(3, WIN, 1, WIN)
    mask = jnp.broadcast_to(mask, (3, WIN, GROUP, WIN)).reshape(3, WIN, GROUP * WIN)
    return bias, mask


def _rope_tables(S):
    rows = S // GRID_W
    row = jnp.repeat(jnp.arange(rows), GRID_W).astype(F32)
    col = jnp.tile(jnp.arange(GRID_W), rows).astype(F32)
    axis_dim = HEAD_DIM // 2
    inv = ROPE_THETA ** (-jnp.arange(0, axis_dim, 2, dtype=F32) / axis_dim)
    ang = jnp.concatenate([row[:, None] * inv, col[:, None] * inv], axis=-1)
    return jnp.cos(ang).T, jnp.sin(ang).T


def _deinterleave_perm():
    within = np.concatenate([np.arange(0, HEAD_DIM, 2), np.arange(1, HEAD_DIM, 2)])
    return within


def _prep_w_in(w_in_l):
    within = _deinterleave_perm()
    cols = np.arange(w_in_l.shape[1])
    base_qb = Q_W + 2 * KV_W
    for h in range(N_Q + N_KV):
        lo = base_qb + h * HEAD_DIM
        cols[lo:lo + HEAD_DIM] = lo + within
    return w_in_l[:, cols].T.astype(BF16)


def _trunk(x, consts, layers, alpha):
    S = x.shape[1]
    cosT, sinT = _rope_tables(S)
    biasT, maskT = consts
    for lw in layers:
        qa, ka, va, qb, kb, vb = _inproj(x, lw["w_inT"], cosT, sinT, lw["qg"], lw["kg"])
        oa = _window_attn(qa, ka, va, biasT, maskT, lw["sink"])
        ob = _global_attn(qb, kb, vb)
        x = _mix_ffn(oa, ob, x, lw, alpha)
    return x


def kernel(x_prompt, x_sample, rel_bias_table, w_in, w_out, attn_sink, q_norm_g, k_norm_g,
           ln1_g, ln1_b, w_gate, w_up, w_down, ln2_g, ln2_b):
    depth = w_in.shape[0]
    alpha = (2.0 * depth) ** 0.25
    within = _deinterleave_perm()
    consts = _window_tables(rel_bias_table)
    layers = []
    for l in range(depth):
        sink = (attn_sink[l].astype(F32) * LOG2E).reshape(N_KV, GROUP, 1)
        sink = jnp.broadcast_to(sink, (N_KV, GROUP, WIN)).reshape(N_KV, 1, GROUP * WIN)
        layers.append(dict(
            w_inT=_prep_w_in(w_in[l]),
            qg=q_norm_g[l].astype(F32)[within].reshape(HEAD_DIM, 1),
            kg=k_norm_g[l].astype(F32)[within].reshape(HEAD_DIM, 1),
            sink=sink,
            w_out=w_out[l].astype(BF16),
            ln1_g=ln1_g[l].astype(F32).reshape(1, -1),
            ln1_b=ln1_b[l].astype(F32).reshape(1, -1),
            w_gate=w_gate[l].astype(BF16),
            w_up=w_up[l].astype(BF16),
            w_down=w_down[l].astype(BF16),
            ln2_g=ln2_g[l].astype(F32).reshape(1, -1),
            ln2_b=ln2_b[l].astype(F32).reshape(1, -1),
        ))
    y_prompt = _trunk(x_prompt, consts, layers, alpha)
    y_sample = _trunk(x_sample, consts, layers, alpha)
    return (y_prompt, y_sample)
```

```python
import functools
import math

import jax
import jax.numpy as jnp
import numpy as np
from jax import lax
from jax.experimental import pallas as pl
from jax.experimental.pallas import tpu as pltpu

F32 = jnp.float32
BF16 = jnp.bfloat16

HEAD_DIM = 64
HALF = HEAD_DIM // 2
N_Q = 8
N_KV = 2
GROUP = N_Q // N_KV
Q_W = N_Q * HEAD_DIM
KV_W = N_KV * HEAD_DIM
WIN = 128
NUM_BUCKETS = 32
MAX_DISTANCE = 128
GRID_W = 64
ROPE_THETA = 10000.0
LN_EPS = 1e-5
RMS_EPS = 1e-6
NEG_INF = -1e30
SCALE = HEAD_DIM ** -0.5
LOG2E = math.log2(math.e)
V_ROWS = 80
CH = 512
ITEMS_PER_TRIP = 16
INPROJ_ROWS = 1024
FFN_ROWS = 2 * CH
WIN_ITEMS_PER_TRIP = 8
V7X_MXU_DIM = 256
V7X_VMEM_BYTES = 64 * 1024 * 1024
VMEM_LIMIT = V7X_VMEM_BYTES - 8 * 1024 * 1024


def _params(n_parallel):
    return pltpu.CompilerParams(
        dimension_semantics=("parallel",) * n_parallel,
        vmem_limit_bytes=VMEM_LIMIT)


def _inproj_kernel(x_ref, w_ref, cos_ref, sin_ref, qg_ref, kg_ref,
                   qa_ref, ka_ref, va_ref, qb_ref, kb_ref, vb_ref):
    tm = x_ref.shape[1]
    xb = x_ref[0].astype(BF16)

    def proj(row0, rows):
        return lax.dot_general(w_ref[row0:row0 + rows, :], xb, (((1,), (1,)), ((), ())),
                               preferred_element_type=F32)

    o_qa, o_ka = 0, Q_W
    o_qb = Q_W + 2 * KV_W
    o_kb = o_qb + Q_W

    def put_v(v_ref, vT):
        w = v_ref.shape[4]
        first_row = lax.broadcasted_iota(jnp.int32, (V_ROWS - HEAD_DIM, w), 0) == 0
        ones_pad = jnp.where(first_row, 1.0, 0.0).astype(BF16)
        for h in range(N_KV):
            for j in range(tm // w):
                v_ref[0, h, j, 0:HEAD_DIM, :] = (
                    vT[h * HEAD_DIM:(h + 1) * HEAD_DIM, j * w:(j + 1) * w].astype(BF16))
                v_ref[0, h, j, HEAD_DIM:V_ROWS, :] = ones_pad

    def put_k(k_ref, kT):
        kr = kT.T
        for h in range(N_KV):
            k_ref[0, h] = kr[:, h * HEAD_DIM:(h + 1) * HEAD_DIM].astype(BF16)

    c = cos_ref[...]
    s = sin_ref[...]

    def norm_rope(xT, g_ref, gain_scale):
        ms = jnp.mean(xT * xT, axis=0, keepdims=True)
        y = xT * lax.rsqrt(ms + RMS_EPS) * (g_ref[...] * gain_scale)
        y0 = y[:HALF]
        y1 = y[HALF:]
        return y0 * c - y1 * s, y0 * s + y1 * c

    heads_per_dot = GROUP
    for h0 in range(0, N_Q, heads_per_dot):
        qb = proj(o_qb + h0 * HEAD_DIM, heads_per_dot * HEAD_DIM)
        for hh in range(heads_per_dot):
            h = h0 + hh
            r0, r1 = norm_rope(qb[hh * HEAD_DIM:(hh + 1) * HEAD_DIM], qg_ref, SCALE * LOG2E)
            for jt in range(tm // CH):
                cols = slice(jt * CH, (jt + 1) * CH)
                qb_ref[0, jt, h * HEAD_DIM:h * HEAD_DIM + HALF, :] = r0[:, cols].astype(BF16)
                qb_ref[0, jt, h * HEAD_DIM + HALF:(h + 1) * HEAD_DIM, :] = r1[:, cols].astype(BF16)
    kvb = proj(o_kb, 2 * KV_W)
    k_rot = []
    for h in range(N_KV):
        r0, r1 = norm_rope(kvb[h * HEAD_DIM:(h + 1) * HEAD_DIM], kg_ref, 1.0)
        k_rot += [r0, r1]
    put_k(kb_ref, jnp.concatenate(k_rot, axis=0))
    put_v(vb_ref, kvb[KV_W:])

    kva = proj(o_ka, 2 * KV_W)
    put_k(ka_ref, kva[:KV_W])
    put_v(va_ref, kva[KV_W:])
    for h in range(N_KV):
        qa_s = (proj(o_qa + h * GROUP * HEAD_DIM, GROUP * HEAD_DIM)
                * (SCALE * LOG2E)).astype(BF16)
        for j in range(tm // WIN):
            for g in range(GROUP):
                qa_ref[0, j, h, :, g * WIN:(g + 1) * WIN] = (
                    qa_s[g * HEAD_DIM:(g + 1) * HEAD_DIM, j * WIN:(j + 1) * WIN])


def _inproj(x, w_inT, cosT, sinT, qg, kg):
    B, S, D = x.shape
    d_in = w_inT.shape[0]
    tm = INPROJ_ROWS
    grid = (B, S // tm)
    q_shape = jax.ShapeDtypeStruct((B, S // WIN, N_KV, HEAD_DIM, GROUP * WIN), BF16)
    k_shape = jax.ShapeDtypeStruct((B, N_KV, S, HEAD_DIM), BF16)
    va_shape = jax.ShapeDtypeStruct((B, N_KV, S // WIN, V_ROWS, WIN), BF16)
    vb_shape = jax.ShapeDtypeStruct((B, N_KV, S // CH, V_ROWS, CH), BF16)
    q_spec = pl.BlockSpec((1, tm // WIN, N_KV, HEAD_DIM, GROUP * WIN),
                          lambda b, i: (b, i, 0, 0, 0))
    qb_shape = jax.ShapeDtypeStruct((B, S // CH, Q_W, CH), BF16)
    qb_spec = pl.BlockSpec((1, tm // CH, Q_W, CH), lambda b, i: (b, i, 0, 0))
    k_spec = pl.BlockSpec((1, N_KV, tm, HEAD_DIM), lambda b, i: (b, 0, i, 0))
    va_spec = pl.BlockSpec((1, N_KV, tm // WIN, V_ROWS, WIN), lambda b, i: (b, 0, i, 0, 0))
    vb_spec = pl.BlockSpec((1, N_KV, tm // CH, V_ROWS, CH), lambda b, i: (b, 0, i, 0, 0))
    return pl.pallas_call(
        _inproj_kernel,
        grid=grid,
        in_specs=[
            pl.BlockSpec((1, tm, D), lambda b, i: (b, i, 0)),
            pl.BlockSpec((d_in, D), lambda b, i: (0, 0)),
            pl.BlockSpec((HALF, tm), lambda b, i: (0, i)),
            pl.BlockSpec((HALF, tm), lambda b, i: (0, i)),
            pl.BlockSpec((HEAD_DIM, 1), lambda b, i: (0, 0)),
            pl.BlockSpec((HEAD_DIM, 1), lambda b, i: (0, 0)),
        ],
        out_specs=[q_spec, k_spec, va_spec, qb_spec, k_spec, vb_spec],
        out_shape=[q_shape, k_shape, va_shape, qb_shape, k_shape, vb_shape],
        compiler_params=_params(2),
        name="inproj",
    )(x, w_inT, cosT, sinT, qg, kg)


def _global_attn_kernel(qT_ref, k_ref, vT_ref, o_ref, s_scr, p_scr, cmax_scr, alpha_scr, acc_scr):
    nq, _, tq = qT_ref.shape[1:]
    n = vT_ref.shape[2]
    per_tile = GROUP * n
    n_items = nq * per_tile
    R = s_scr.shape[0]
    assert n % R == 0 or R % n == 0
    trips_per_head = max(n // R, 1)
    slots_per_head = min(n, R)

    def decode(item):
        rem = item % per_tile
        return item // per_tile, rem // n, rem % n

    def head_rows(g):
        return pl.ds(pl.multiple_of(g * HEAD_DIM, HEAD_DIM), HEAD_DIM)

    def scores(slot, item):
        qt, g, c = decode(item)
        kc = k_ref[0, 0, pl.ds(pl.multiple_of(c * CH, CH), CH), :]
        s = jnp.dot(kc, qT_ref[0, qt, head_rows(g), :], preferred_element_type=F32)
        s_scr[slot] = s
        cmax_scr[slot] = jnp.max(s, axis=0, keepdims=True)

    def probs(slot, m_prev):
        m = jnp.maximum(m_prev, cmax_scr[slot])
        alpha_scr[slot] = jnp.exp2(m_prev - m)
        p_scr[slot] = jnp.exp2(s_scr[slot] - m).astype(BF16)
        return m

    def accumulate(slot, item, acc):
        _, _, c = decode(jnp.maximum(item, 0))
        return acc * alpha_scr[slot] + jnp.dot(vT_ref[0, 0, c], p_scr[slot],
                                               preferred_element_type=F32)

    def emit(slot, item, acc):
        if (slot + 1) % slots_per_head == 0:
            qt, g, _ = decode(jnp.maximum(item, 0))
            o_ref[0, qt, head_rows(g), :] = (
                acc[:HEAD_DIM] * (1.0 / acc[HEAD_DIM:HEAD_DIM + 1])).astype(BF16)

    def trip(u, m, with_scores):
        t0 = R * u
        acc = acc_scr[...]
        for j in range(R):
            if j % slots_per_head == 0:
                m = jnp.where(u % trips_per_head == 0, NEG_INF, m)
            acc = accumulate(j, t0 - R + j, acc)
            emit(j, t0 - R + j, acc)
            m = probs(j, m)
            if with_scores:
                scores(j, t0 + R + j)
        acc_scr[...] = acc
        return m

    n_trips = n_items // R
    for j in range(R):
        scores(j, j)
    p_scr[...] = jnp.zeros(p_scr.shape, BF16)
    alpha_scr[...] = jnp.ones(alpha_scr.shape, F32)
    acc_scr[...] = jnp.ones(acc_scr.shape, F32)
    m = lax.fori_loop(0, n_trips - 1, lambda u, m: trip(u, m, True),
                      jnp.full((1, tq), NEG_INF, F32))
    trip(n_trips - 1, m, False)
    acc = acc_scr[...]
    for j in range(R):
        acc = accumulate(j, n_items - R + j, acc)
        emit(j, n_items - R + j, acc)


def _global_attn(qT, k, vT):
    B, nq, _, tq = qT.shape
    S = k.shape[2]
    gw = GROUP * HEAD_DIM
    R = ITEMS_PER_TRIP
    return pl.pallas_call(
        _global_attn_kernel,
        grid=(B, N_KV),
        in_specs=[
            pl.BlockSpec((1, nq, gw, tq), lambda b, h: (b, 0, h, 0)),
            pl.BlockSpec((1, 1, S, HEAD_DIM), lambda b, h: (b, h, 0, 0)),
            pl.BlockSpec((1, 1, S // CH, V_ROWS, CH), lambda b, h: (b, h, 0, 0, 0)),
        ],
        out_specs=pl.BlockSpec((1, nq, gw, tq), lambda b, h: (b, 0, h, 0)),
        out_shape=jax.ShapeDtypeStruct((B, nq, Q_W, tq), BF16),
        scratch_shapes=[pltpu.VMEM((R, CH, tq), F32),
                        pltpu.VMEM((R, CH, tq), BF16),
                        pltpu.VMEM((R, 1, tq), F32),
                        pltpu.VMEM((R, 1, tq), F32),
                        pltpu.VMEM((V_ROWS, tq), F32)],
        compiler_params=_params(2),
        name="global_attn",
    )(qT, k, vT)


def _window_attn_kernel(q_ref, k_ref, vT_ref, bias_ref, mask_ref, sink_ref, o_ref,
                        s_scr, p_scr, m_scr, sinkp_scr):
    nb = q_ref.shape[1]
    R = WIN_ITEMS_PER_TRIP
    assert nb % R == 0
    n_trips = nb // R
    sink = sink_ref[0]

    def key_block(i, c):
        kb = i - 1 + c
        return kb, jnp.clip(kb, 0, nb - 1)

    def scores(slot, i):
        q = q_ref[0, i, 0]
        m = sink
        for c in range(3):
            kb, kb_c = key_block(i, c)
            kc = k_ref[0, 0, pl.ds(pl.multiple_of(kb_c * WIN, WIN), WIN), :]
            s = jnp.dot(kc, q, preferred_element_type=F32) + bias_ref[0, c]
            if c != 1:
                thr = jnp.where(kb == kb_c, 0.5, 2.0)
                s = jnp.where(mask_ref[c] > thr, s, NEG_INF)
            s_scr[slot, c] = s
            m = jnp.maximum(m, jnp.max(s, axis=0, keepdims=True))
        m_scr[slot] = m

    def probs(slot):
        m = m_scr[slot]
        sinkp_scr[slot] = jnp.exp2(sink - m)
        for c in range(3):
            p_scr[slot, c] = jnp.exp2(s_scr[slot, c] - m).astype(BF16)

    def output(slot, i):
        i = jnp.maximum(i, 0)
        acc = None
        for c in range(3):
            _, kb_c = key_block(i, c)
            pv = jnp.dot(vT_ref[0, 0, kb_c], p_scr[slot, c], preferred_element_type=F32)
            acc = pv if acc is None else acc + pv
        denom = acc[HEAD_DIM:HEAD_DIM + 1] + sinkp_scr[slot]
        oT = (acc[:HEAD_DIM] * (1.0 / denom)).astype(BF16)
        for g in range(GROUP):
            o_ref[0, i, g * HEAD_DIM:(g + 1) * HEAD_DIM, :] = oT[:, g * WIN:(g + 1) * WIN]

    def trip(u, with_scores):
        t0 = R * u
        for j in range(R):
            output(j, t0 - R + j)
            probs(j)
            if with_scores:
                scores(j, t0 + R + j)

    for j in range(R):
        scores(j, j)
    p_scr[...] = jnp.zeros(p_scr.shape, BF16)
    sinkp_scr[...] = jnp.ones(sinkp_scr.shape, F32)

    def body(u, carry):
        trip(u, True)
        return carry

    lax.fori_loop(0, n_trips - 1, body, 0)
    trip(n_trips - 1, False)
    for j in range(R):
        output(j, nb - R + j)


def _window_attn(q, k, vT, biasT, maskT, sink_row):
    B, nb = q.shape[:2]
    S = k.shape[2]
    gw = GROUP * HEAD_DIM
    gl = GROUP * WIN
    R = WIN_ITEMS_PER_TRIP
    return pl.pallas_call(
        _window_attn_kernel,
        grid=(B, N_KV),
        in_specs=[
            pl.BlockSpec((1, nb, 1, HEAD_DIM, gl), lambda b, h: (b, 0, h, 0, 0)),
            pl.BlockSpec((1, 1, S, HEAD_DIM), lambda b, h: (b, h, 0, 0)),
            pl.BlockSpec((1, 1, nb, V_ROWS, WIN), lambda b, h: (b, h, 0, 0, 0)),
            pl.BlockSpec((1, 3, WIN, gl), lambda b, h: (h, 0, 0, 0)),
            pl.BlockSpec((3, WIN, gl), lambda b, h: (0, 0, 0)),
            pl.BlockSpec((1, 1, gl), lambda b, h: (h, 0, 0)),
        ],
        out_specs=pl.BlockSpec((1, nb, gw, WIN), lambda b, h: (b, 0, h, 0)),
        out_shape=jax.ShapeDtypeStruct((B, nb, Q_W, WIN), BF16),
        scratch_shapes=[pltpu.VMEM((R, 3, WIN, gl), F32),
                        pltpu.VMEM((R, 3, WIN, gl), BF16),
                        pltpu.VMEM((R, 1, gl), F32),
                        pltpu.VMEM((R, 1, gl), F32)],
        compiler_params=_params(2),
        name="window_attn",
    )(q, k, vT, biasT, maskT, sink_row)


def _layer_norm(v, g, b):
    mu = jnp.mean(v, axis=-1, keepdims=True)
    d = v - mu
    var = jnp.mean(d * d, axis=-1, keepdims=True)
    return d * lax.rsqrt(var + LN_EPS) * g + b


def _mix_ffn_kernel(oa_ref, ob_ref, x_ref, wo_ref, g1_ref, b1_ref, wg_ref, wu_ref, wd_ref,
                    g2_ref, b2_ref, y_ref, *, alpha, ff_chunk):
    d_ff = wg_ref.shape[1]
    wins_per_group = CH // WIN

    def mixed_norm(r):
        oaT = jnp.concatenate([oa_ref[0, r * wins_per_group + j] for j in range(wins_per_group)],
                              axis=1)
        oT = jnp.concatenate([oaT, ob_ref[0, r]], axis=0)
        mixed = lax.dot_general(oT, wo_ref[...], (((0,), (0,)), ((), ())),
                                preferred_element_type=F32)
        return _layer_norm(alpha * x_ref[0, r * CH:(r + 1) * CH, :] + mixed,
                           g1_ref[...], b1_ref[...])

    def zero_after(v):
        return jnp.sum(v, axis=0, keepdims=True) * 0.0

    def ffn(x1, first_after, last_after):
        xb = x1.astype(BF16)
        n_chunks = d_ff // ff_chunk
        acc = None
        for ci in range(n_chunks):
            c0 = ci * ff_chunk
            lhs = xb
            if ci == 0 and first_after is not None:
                lhs = (x1 + first_after).astype(BF16)
            if ci == n_chunks - 1 and last_after is not None:
                lhs = (x1 + last_after).astype(BF16)
            gate = jnp.dot(lhs, wg_ref[:, c0:c0 + ff_chunk], preferred_element_type=F32)
            up = jnp.dot(lhs, wu_ref[:, c0:c0 + ff_chunk], preferred_element_type=F32)
            h = (gate * jax.nn.sigmoid(gate) * up).astype(BF16)
            part = jnp.dot(h, wd_ref[c0:c0 + ff_chunk, :], preferred_element_type=F32)
            acc = part if acc is None else acc + part
        return acc

    x1a = mixed_norm(0)
    x1b = mixed_norm(1)
    acc_a = ffn(x1a, None, zero_after(x1b))
    y_ref[0, 0:CH, :] = _layer_norm(alpha * x1a + acc_a, g2_ref[...], b2_ref[...])
    acc_b = ffn(x1b, zero_after(acc_a), None)
    y_ref[0, CH:2 * CH, :] = _layer_norm(alpha * x1b + acc_b, g2_ref[...], b2_ref[...])


def _mix_ffn(oa, ob, x, lw, alpha):
    B, S, D = x.shape
    d_ff = lw["w_gate"].shape[1]
    tm = FFN_ROWS
    ff_chunk = V7X_MXU_DIM
    assert d_ff % ff_chunk == 0 and tm == 2 * CH
    row = lambda bb, i: (bb, i, 0)
    const = lambda bb, i: (0, 0)
    once = pl.Buffered(1)
    vec = pl.BlockSpec((1, D), const)
    return pl.pallas_call(
        functools.partial(_mix_ffn_kernel, alpha=alpha, ff_chunk=ff_chunk),
        grid=(B, S // tm),
        in_specs=[
            pl.BlockSpec((1, tm // WIN, Q_W, WIN), lambda bb, i: (bb, i, 0, 0)),
            pl.BlockSpec((1, tm // CH, Q_W, CH), lambda bb, i: (bb, i, 0, 0)),
            pl.BlockSpec((1, tm, D), row),
            pl.BlockSpec((2 * Q_W, D), const, pipeline_mode=once),
            vec, vec,
            pl.BlockSpec((D, d_ff), const, pipeline_mode=once),
            pl.BlockSpec((D, d_ff), const, pipeline_mode=once),
            pl.BlockSpec((d_ff, D), const, pipeline_mode=once),
            vec, vec,
        ],
        out_specs=pl.BlockSpec((1, tm, D), row),
        out_shape=jax.ShapeDtypeStruct((B, S, D), F32),
        compiler_params=_params(2),
        name="mix_ffn",
    )(oa, ob, x, lw["w_out"], lw["ln1_g"], lw["ln1_b"], lw["w_gate"], lw["w_up"], lw["w_down"],
      lw["ln2_g"], lw["ln2_b"])


def _t5_bucket(rel):
    nb = NUM_BUCKETS // 2
    max_exact = nb // 2
    bucket = (rel > 0).astype(jnp.int32) * nb
    n = jnp.abs(rel)
    n_f = jnp.maximum(n, 1).astype(F32)
    large = max_exact + (jnp.log(n_f / max_exact) / math.log(MAX_DISTANCE / max_exact)
                         * (nb - max_exact)).astype(jnp.int32)
    large = jnp.minimum(large, nb - 1)
    return bucket + jnp.where(n < max_exact, n, large)


def _window_tables(rel_table):
    key = jnp.arange(3 * WIN)[:, None]
    qry = jnp.arange(WIN)[None, :]
    rel = (key - WIN) - qry
    onehot = (_t5_bucket(rel)[..., None] == jnp.arange(NUM_BUCKETS)).astype(F32)
    bias = jnp.einsum("kqb,bh->kqh", onehot, rel_table.astype(F32),
                      precision=lax.Precision.HIGHEST)
    bias = bias.reshape(3, WIN, WIN, N_KV, GROUP).transpose(3, 0, 1, 4, 2)
    bias = bias.reshape(N_KV, 3, WIN, GROUP * WIN) * LOG2E
    mask = (jnp.abs(rel) <= WIN).astype(F32).reshape(3, WIN, 1, WIN)
    mask = jnp.broadcast_to(mask, (3, WIN, GROUP, WIN)).reshape(3, WIN, GROUP * WIN)
    return bias, mask


def _rope_tables(S):
    rows = S // GRID_W
    row = jnp.repeat(jnp.arange(rows), GRID_W).astype(F32)
    col = jnp.tile(jnp.arange(GRID_W), rows).astype(F32)
    axis_dim = HEAD_DIM // 2
    inv = ROPE_THETA ** (-jnp.arange(0, axis_dim, 2, dtype=F32) / axis_dim)
    ang = jnp.concatenate([row[:, None] * inv, col[:, None] * inv], axis=-1)
    return jnp.cos(ang).T, jnp.sin(ang).T


def _deinterleave_perm():
    within = np.concatenate([np.arange(0, HEAD_DIM, 2), np.arange(1, HEAD_DIM, 2)])
    return within


def _prep_w_in(w_in_l):
    within = _deinterleave_perm()
    cols = np.arange(w_in_l.shape[1])
    base_qb = Q_W + 2 * KV_W
    for h in range(N_Q + N_KV):
        lo = base_qb + h * HEAD_DIM
        cols[lo:lo + HEAD_DIM] = lo + within
    return w_in_l[:, cols].T.astype(BF16)


def _trunk(x, consts, layers, alpha):
    S = x.shape[1]
    cosT, sinT = _rope_tables(S)
    biasT, maskT = consts
    for lw in layers:
        qa, ka, va, qb, kb, vb = _inproj(x, lw["w_inT"], cosT, sinT, lw["qg"], lw["kg"])
        oa = _window_attn(qa, ka, va, biasT, maskT, lw["sink"])
        ob = _global_attn(qb, kb, vb)
        x = _mix_ffn(oa, ob, x, lw, alpha)
    return x


def kernel(x_prompt, x_sample, rel_bias_table, w_in, w_out, attn_sink, q_norm_g, k_norm_g,
           ln1_g, ln1_b, w_gate, w_up, w_down, ln2_g, ln2_b):
    depth = w_in.shape[0]
    alpha = (2.0 * depth) ** 0.25
    within = _deinterleave_perm()
    consts = _window_tables(rel_bias_table)
    layers = []
    for l in range(depth):
        sink = (attn_sink[l].astype(F32) * LOG2E).reshape(N_KV, GROUP, 1)
        sink = jnp.broadcast_to(sink, (N_KV, GROUP, WIN)).reshape(N_KV, 1, GROUP * WIN)
        layers.append(dict(
            w_inT=_prep_w_in(w_in[l]),
            qg=q_norm_g[l].astype(F32)[within].reshape(HEAD_DIM, 1),
            kg=k_norm_g[l].astype(F32)[within].reshape(HEAD_DIM, 1),
            sink=sink,
            w_out=w_out[l].astype(BF16),
            ln1_g=ln1_g[l].astype(F32).reshape(1, -1),
            ln1_b=ln1_b[l].astype(F32).reshape(1, -1),
            w_gate=w_gate[l].astype(BF16),
            w_up=w_up[l].astype(BF16),
            w_down=w_down[l].astype(BF16),
            ln2_g=ln2_g[l].astype(F32).reshape(1, -1),
            ln2_b=ln2_b[l].astype(F32).reshape(1, -1),
        ))
    y_prompt = _trunk(x_prompt, consts, layers, alpha)
    y_sample = _trunk(x_sample, consts, layers, alpha)
    return (y_prompt, y_sample)
```

```python
import functools
import math

import jax
import jax.numpy as jnp
import numpy as np
from jax import lax
from jax.experimental import pallas as pl
from jax.experimental.pallas import tpu as pltpu

F32 = jnp.float32
BF16 = jnp.bfloat16

HEAD_DIM = 64
HALF = HEAD_DIM // 2
N_Q = 8
N_KV = 2
GROUP = N_Q // N_KV
Q_W = N_Q * HEAD_DIM
KV_W = N_KV * HEAD_DIM
WIN = 128
NUM_BUCKETS = 32
MAX_DISTANCE = 128
GRID_W = 64
ROPE_THETA = 10000.0
LN_EPS = 1e-5
RMS_EPS = 1e-6
NEG_INF = -1e30
SCALE = HEAD_DIM ** -0.5
LOG2E = math.log2(math.e)
V7X_MXU_DIM = 256
V7X_VMEM_BYTES = 64 * 1024 * 1024
VMEM_LIMIT = V7X_VMEM_BYTES - 8 * 1024 * 1024
SPILL_HEADROOM = 6 * 1024 * 1024
V_ROWS = 80
CH = 512
KCH = V7X_MXU_DIM
ITEMS_PER_TRIP = 32
INPROJ_ROWS = 1024
FFN_ROWS = 2 * CH
WIN_ITEMS_PER_TRIP = 8


def _params(n_parallel):
    return pltpu.CompilerParams(
        dimension_semantics=("parallel",) * n_parallel,
        vmem_limit_bytes=VMEM_LIMIT)


def _inproj_kernel(x_ref, w_ref, cos_ref, sin_ref, qg_ref, kg_ref,
                   qa_ref, ka_ref, va_ref, qb_ref, kb_ref, vb_ref):
    tm = x_ref.shape[1]
    xb = x_ref[0].astype(BF16)

    def proj(row0, rows):
        return lax.dot_general(w_ref[row0:row0 + rows, :], xb, (((1,), (1,)), ((), ())),
                               preferred_element_type=F32)

    o_qa, o_ka = 0, Q_W
    o_qb = Q_W + 2 * KV_W
    o_kb = o_qb + Q_W

    def put_v(v_ref, vT):
        w = v_ref.shape[4]
        first_row = lax.broadcasted_iota(jnp.int32, (V_ROWS - HEAD_DIM, w), 0) == 0
        ones_pad = jnp.where(first_row, 1.0, 0.0).astype(BF16)
        for h in range(N_KV):
            for j in range(tm // w):
                v_ref[0, h, j, 0:HEAD_DIM, :] = (
                    vT[h * HEAD_DIM:(h + 1) * HEAD_DIM, j * w:(j + 1) * w].astype(BF16))
                v_ref[0, h, j, HEAD_DIM:V_ROWS, :] = ones_pad

    def put_k(k_ref, kT):
        kr = kT.T
        for h in range(N_KV):
            k_ref[0, h] = kr[:, h * HEAD_DIM:(h + 1) * HEAD_DIM].astype(BF16)

    c = cos_ref[...]
    s = sin_ref[...]

    def norm_rope(xT, g_ref, gain_scale):
        ms = jnp.mean(xT * xT, axis=0, keepdims=True)
        y = xT * lax.rsqrt(ms + RMS_EPS) * (g_ref[...] * gain_scale)
        y0 = y[:HALF]
        y1 = y[HALF:]
        return y0 * c - y1 * s, y0 * s + y1 * c

    heads_per_dot = GROUP
    for h0 in range(0, N_Q, heads_per_dot):
        qb = proj(o_qb + h0 * HEAD_DIM, heads_per_dot * HEAD_DIM)
        for hh in range(heads_per_dot):
            h = h0 + hh
            r0, r1 = norm_rope(qb[hh * HEAD_DIM:(hh + 1) * HEAD_DIM], qg_ref, SCALE * LOG2E)
            for jt in range(tm // CH):
                cols = slice(jt * CH, (jt + 1) * CH)
                qb_ref[0, jt, h * HEAD_DIM:h * HEAD_DIM + HALF, :] = r0[:, cols].astype(BF16)
                qb_ref[0, jt, h * HEAD_DIM + HALF:(h + 1) * HEAD_DIM, :] = r1[:, cols].astype(BF16)
    kvb = proj(o_kb, 2 * KV_W)
    k_rot = []
    for h in range(N_KV):
        r0, r1 = norm_rope(kvb[h * HEAD_DIM:(h + 1) * HEAD_DIM], kg_ref, 1.0)
        k_rot += [r0, r1]
    put_k(kb_ref, jnp.concatenate(k_rot, axis=0))
    put_v(vb_ref, kvb[KV_W:])

    kva = proj(o_ka, 2 * KV_W)
    put_k(ka_ref, kva[:KV_W])
    put_v(va_ref, kva[KV_W:])
    for h in range(N_KV):
        qa_s = (proj(o_qa + h * GROUP * HEAD_DIM, GROUP * HEAD_DIM)
                * (SCALE * LOG2E)).astype(BF16)
        for j in range(tm // WIN):
            for g in range(GROUP):
                qa_ref[0, j, h, :, g * WIN:(g + 1) * WIN] = (
                    qa_s[g * HEAD_DIM:(g + 1) * HEAD_DIM, j * WIN:(j + 1) * WIN])


def _inproj(x, w_inT, cosT, sinT, qg, kg):
    B, S, D = x.shape
    d_in = w_inT.shape[0]
    tm = INPROJ_ROWS
    grid = (B, S // tm)
    q_shape = jax.ShapeDtypeStruct((B, S // WIN, N_KV, HEAD_DIM, GROUP * WIN), BF16)
    k_shape = jax.ShapeDtypeStruct((B, N_KV, S, HEAD_DIM), BF16)
    va_shape = jax.ShapeDtypeStruct((B, N_KV, S // WIN, V_ROWS, WIN), BF16)
    vb_shape = jax.ShapeDtypeStruct((B, N_KV, S // KCH, V_ROWS, KCH), BF16)
    q_spec = pl.BlockSpec((1, tm // WIN, N_KV, HEAD_DIM, GROUP * WIN),
                          lambda b, i: (b, i, 0, 0, 0))
    qb_shape = jax.ShapeDtypeStruct((B, S // CH, Q_W, CH), BF16)
    qb_spec = pl.BlockSpec((1, tm // CH, Q_W, CH), lambda b, i: (b, i, 0, 0))
    k_spec = pl.BlockSpec((1, N_KV, tm, HEAD_DIM), lambda b, i: (b, 0, i, 0))
    va_spec = pl.BlockSpec((1, N_KV, tm // WIN, V_ROWS, WIN), lambda b, i: (b, 0, i, 0, 0))
    vb_spec = pl.BlockSpec((1, N_KV, tm // KCH, V_ROWS, KCH), lambda b, i: (b, 0, i, 0, 0))
    return pl.pallas_call(
        _inproj_kernel,
        grid=grid,
        in_specs=[
            pl.BlockSpec((1, tm, D), lambda b, i: (b, i, 0)),
            pl.BlockSpec((d_in, D), lambda b, i: (0, 0)),
            pl.BlockSpec((HALF, tm), lambda b, i: (0, i)),
            pl.BlockSpec((HALF, tm), lambda b, i: (0, i)),
            pl.BlockSpec((HEAD_DIM, 1), lambda b, i: (0, 0)),
            pl.BlockSpec((HEAD_DIM, 1), lambda b, i: (0, 0)),
        ],
        out_specs=[q_spec, k_spec, va_spec, qb_spec, k_spec, vb_spec],
        out_shape=[q_shape, k_shape, va_shape, qb_shape, k_shape, vb_shape],
        compiler_params=_params(2),
        name="inproj",
    )(x, w_inT, cosT, sinT, qg, kg)


def _global_attn_kernel(qT_ref, k_ref, vT_ref, o_ref, s_scr, p_scr, cmax_scr, alpha_scr, acc_scr):
    nq, _, tq = qT_ref.shape[1:]
    n = vT_ref.shape[2]
    per_tile = k_ref.shape[1] * GROUP * n
    n_items = nq * per_tile
    R = s_scr.shape[0]
    assert n % R == 0 or R % n == 0
    trips_per_head = max(n // R, 1)
    slots_per_head = min(n, R)

    def decode(item):
        rem = item % per_tile
        return item // per_tile, rem // n, rem % n

    def head_rows(g):
        return pl.ds(pl.multiple_of(g * HEAD_DIM, HEAD_DIM), HEAD_DIM)

    def scores(slot, item):
        qt, g, c = decode(item)
        kc = k_ref[0, g // GROUP, pl.ds(pl.multiple_of(c * KCH, KCH), KCH), :]
        s = jnp.dot(kc, qT_ref[0, qt, head_rows(g), :], preferred_element_type=F32)
        s_scr[slot] = s
        cmax_scr[slot] = jnp.max(s, axis=0, keepdims=True)

    def probs(slot, m_prev):
        m = jnp.maximum(m_prev, cmax_scr[slot])
        alpha_scr[slot] = jnp.exp2(m_prev - m)
        p_scr[slot] = jnp.exp2(s_scr[slot] - m).astype(BF16)
        return m

    def accumulate(slot, item, acc):
        _, g, c = decode(jnp.maximum(item, 0))
        return acc * alpha_scr[slot] + jnp.dot(vT_ref[0, g // GROUP, c], p_scr[slot],
                                               preferred_element_type=F32)

    def emit(slot, item, acc):
        if (slot + 1) % slots_per_head == 0:
            qt, g, _ = decode(jnp.maximum(item, 0))
            o_ref[0, qt, head_rows(g), :] = (
                acc[:HEAD_DIM] * (1.0 / acc[HEAD_DIM:HEAD_DIM + 1])).astype(BF16)

    def trip(u, m, with_scores):
        t0 = R * u
        acc = acc_scr[...]
        for j in range(R):
            if j % slots_per_head == 0:
                m = jnp.where(u % trips_per_head == 0, NEG_INF, m)
            acc = accumulate(j, t0 - R + j, acc)
            emit(j, t0 - R + j, acc)
            m = probs(j, m)
            if with_scores:
                scores(j, t0 + R + j)
        acc_scr[...] = acc
        return m

    n_trips = n_items // R
    for j in range(R):
        scores(j, j)
    p_scr[...] = jnp.zeros(p_scr.shape, BF16)
    alpha_scr[...] = jnp.ones(alpha_scr.shape, F32)
    acc_scr[...] = jnp.ones(acc_scr.shape, F32)
    m = lax.fori_loop(0, n_trips - 1, lambda u, m: trip(u, m, True),
                      jnp.full((1, tq), NEG_INF, F32))
    trip(n_trips - 1, m, False)
    acc = acc_scr[...]
    for j in range(R):
        acc = accumulate(j, n_items - R + j, acc)
        emit(j, n_items - R + j, acc)


def _global_attn(qT, k, vT):
    B, nq, _, tq = qT.shape
    S = k.shape[2]
    R = ITEMS_PER_TRIP
    lanes = 128
    per_kv_head = 2 * 2 * (2 * S * GROUP * HEAD_DIM + S * lanes + S * V_ROWS)
    scratch = R * KCH * tq * (4 + 2) + V_ROWS * tq * 4
    hk = N_KV if N_KV * per_kv_head + scratch <= VMEM_LIMIT - SPILL_HEADROOM else 1
    gw = hk * GROUP * HEAD_DIM
    return pl.pallas_call(
        _global_attn_kernel,
        grid=(B, N_KV // hk),
        in_specs=[
            pl.BlockSpec((1, nq, gw, tq), lambda b, h: (b, 0, h, 0)),
            pl.BlockSpec((1, hk, S, HEAD_DIM), lambda b, h: (b, h, 0, 0)),
            pl.BlockSpec((1, hk, S // KCH, V_ROWS, KCH), lambda b, h: (b, h, 0, 0, 0)),
        ],
        out_specs=pl.BlockSpec((1, nq, gw, tq), lambda b, h: (b, 0, h, 0)),
        out_shape=jax.ShapeDtypeStruct((B, nq, Q_W, tq), BF16),
        scratch_shapes=[pltpu.VMEM((R, KCH, tq), F32),
                        pltpu.VMEM((R, KCH, tq), BF16),
                        pltpu.VMEM((R, 1, tq), F32),
                        pltpu.VMEM((R, 1, tq), F32),
                        pltpu.VMEM((V_ROWS, tq), F32)],
        compiler_params=_params(2),
        name="global_attn",
    )(qT, k, vT)


def _window_attn_kernel(q_ref, k_ref, vT_ref, bias_ref, mask_ref, sink_ref, o_ref,
                        s_scr, p_scr, m_scr, sinkp_scr):
    nb = q_ref.shape[1]
    R = WIN_ITEMS_PER_TRIP
    assert nb % R == 0
    n_trips = nb // R
    sink = sink_ref[0]

    def key_block(i, c):
        kb = i - 1 + c
        return kb, jnp.clip(kb, 0, nb - 1)

    def scores(slot, i):
        q = q_ref[0, i, 0]
        m = sink
        for c in range(3):
            kb, kb_c = key_block(i, c)
            kc = k_ref[0, 0, pl.ds(pl.multiple_of(kb_c * WIN, WIN), WIN), :]
            s = jnp.dot(kc, q, preferred_element_type=F32) + bias_ref[0, c]
            if c != 1:
                thr = jnp.where(kb == kb_c, 0.5, 2.0)
                s = jnp.where(mask_ref[c] > thr, s, NEG_INF)
            s_scr[slot, c] = s
            m = jnp.maximum(m, jnp.max(s, axis=0, keepdims=True))
        m_scr[slot] = m

    def probs(slot):
        m = m_scr[slot]
        sinkp_scr[slot] = jnp.exp2(sink - m)
        for c in range(3):
            p_scr[slot, c] = jnp.exp2(s_scr[slot, c] - m).astype(BF16)

    def output(slot, i):
        i = jnp.maximum(i, 0)
        acc = None
        for c in range(3):
            _, kb_c = key_block(i, c)
            pv = jnp.dot(vT_ref[0, 0, kb_c], p_scr[slot, c], preferred_element_type=F32)
            acc = pv if acc is None else acc + pv
        denom = acc[HEAD_DIM:HEAD_DIM + 1] + sinkp_scr[slot]
        oT = (acc[:HEAD_DIM] * (1.0 / denom)).astype(BF16)
        for g in range(GROUP):
            o_ref[0, i, g * HEAD_DIM:(g + 1) * HEAD_DIM, :] = oT[:, g * WIN:(g + 1) * WIN]

    def trip(u, with_scores):
        t0 = R * u
        for j in range(R):
            output(j, t0 - R + j)
            probs(j)
            if with_scores:
                scores(j, t0 + R + j)

    for j in range(R):
        scores(j, j)
    p_scr[...] = jnp.zeros(p_scr.shape, BF16)
    sinkp_scr[...] = jnp.ones(sinkp_scr.shape, F32)

    def body(u, carry):
        trip(u, True)
        return carry

    lax.fori_loop(0, n_trips - 1, body, 0)
    trip(n_trips - 1, False)
    for j in range(R):
        output(j, nb - R + j)


def _window_attn(q, k, vT, biasT, maskT, sink_row):
    B, nb = q.shape[:2]
    S = k.shape[2]
    gw = GROUP * HEAD_DIM
    gl = GROUP * WIN
    R = WIN_ITEMS_PER_TRIP
    return pl.pallas_call(
        _window_attn_kernel,
        grid=(B, N_KV),
        in_specs=[
            pl.BlockSpec((1, nb, 1, HEAD_DIM, gl), lambda b, h: (b, 0, h, 0, 0)),
            pl.BlockSpec((1, 1, S, HEAD_DIM), lambda b, h: (b, h, 0, 0)),
            pl.BlockSpec((1, 1, nb, V_ROWS, WIN), lambda b, h: (b, h, 0, 0, 0)),
            pl.BlockSpec((1, 3, WIN, gl), lambda b, h: (h, 0, 0, 0)),
            pl.BlockSpec((3, WIN, gl), lambda b, h: (0, 0, 0)),
            pl.BlockSpec((1, 1, gl), lambda b, h: (h, 0, 0)),
        ],
        out_specs=pl.BlockSpec((1, nb, gw, WIN), lambda b, h: (b, 0, h, 0)),
        out_shape=jax.ShapeDtypeStruct((B, nb, Q_W, WIN), BF16),
        scratch_shapes=[pltpu.VMEM((R, 3, WIN, gl), F32),
                        pltpu.VMEM((R, 3, WIN, gl), BF16),
                        pltpu.VMEM((R, 1, gl), F32),
                        pltpu.VMEM((R, 1, gl), F32)],
        compiler_params=_params(2),
        name="window_attn",
    )(q, k, vT, biasT, maskT, sink_row)


def _layer_norm(v, g, b):
    mu = jnp.mean(v, axis=-1, keepdims=True)
    d = v - mu
    var = jnp.mean(d * d, axis=-1, keepdims=True)
    return d * lax.rsqrt(var + LN_EPS) * g + b


def _mix_ffn_kernel(oa_ref, ob_ref, x_ref, wo_ref, g1_ref, b1_ref, wg_ref, wu_ref, wd_ref,
                    g2_ref, b2_ref, y_ref, *, alpha, ff_chunk):
    d_ff = wg_ref.shape[1]
    wins_per_group = CH // WIN

    def mixed_norm(r):
        oaT = jnp.concatenate([oa_ref[0, r * wins_per_group + j] for j in range(wins_per_group)],
                              axis=1)
        oT = jnp.concatenate([oaT, ob_ref[0, r]], axis=0)
        mixed = lax.dot_general(oT, wo_ref[...], (((0,), (0,)), ((), ())),
                                preferred_element_type=F32)
        return _layer_norm(alpha * x_ref[0, r * CH:(r + 1) * CH, :] + mixed,
                           g1_ref[...], b1_ref[...])

    def zero_after(v):
        return jnp.sum(v, axis=0, keepdims=True) * 0.0

    def ffn(x1, first_after, last_after):
        xb = x1.astype(BF16)
        n_chunks = d_ff // ff_chunk
        acc = None
        for ci in range(n_chunks):
            c0 = ci * ff_chunk
            lhs = xb
            if ci == 0 and first_after is not None:
                lhs = (x1 + first_after).astype(BF16)
            if ci == n_chunks - 1 and last_after is not None:
                lhs = (x1 + last_after).astype(BF16)
            gate = jnp.dot(lhs, wg_ref[:, c0:c0 + ff_chunk], preferred_element_type=F32)
            up = jnp.dot(lhs, wu_ref[:, c0:c0 + ff_chunk], preferred_element_type=F32)
            h = (gate * jax.nn.sigmoid(gate) * up).astype(BF16)
            part = jnp.dot(h, wd_ref[c0:c0 + ff_chunk, :], preferred_element_type=F32)
            acc = part if acc is None else acc + part
        return acc

    x1a = mixed_norm(0)
    x1b = mixed_norm(1)
    acc_a = ffn(x1a, None, zero_after(x1b))
    y_ref[0, 0:CH, :] = _layer_norm(alpha * x1a + acc_a, g2_ref[...], b2_ref[...])
    acc_b = ffn(x1b, zero_after(acc_a), None)
    y_ref[0, CH:2 * CH, :] = _layer_norm(alpha * x1b + acc_b, g2_ref[...], b2_ref[...])


def _mix_ffn(oa, ob, x, lw, alpha):
    B, S, D = x.shape
    d_ff = lw["w_gate"].shape[1]
    tm = FFN_ROWS
    ff_chunk = V7X_MXU_DIM
    assert d_ff % ff_chunk == 0 and tm == 2 * CH
    row = lambda bb, i: (bb, i, 0)
    const = lambda bb, i: (0, 0)
    once = pl.Buffered(1)
    vec = pl.BlockSpec((1, D), const)
    return pl.pallas_call(
        functools.partial(_mix_ffn_kernel, alpha=alpha, ff_chunk=ff_chunk),
        grid=(B, S // tm),
        in_specs=[
            pl.BlockSpec((1, tm // WIN, Q_W, WIN), lambda bb, i: (bb, i, 0, 0)),
            pl.BlockSpec((1, tm // CH, Q_W, CH), lambda bb, i: (bb, i, 0, 0)),
            pl.BlockSpec((1, tm, D), row),
            pl.BlockSpec((2 * Q_W, D), const, pipeline_mode=once),
            vec, vec,
            pl.BlockSpec((D, d_ff), const, pipeline_mode=once),
            pl.BlockSpec((D, d_ff), const, pipeline_mode=once),
            pl.BlockSpec((d_ff, D), const, pipeline_mode=once),
            vec, vec,
        ],
        out_specs=pl.BlockSpec((1, tm, D), row),
        out_shape=jax.ShapeDtypeStruct((B, S, D), F32),
        compiler_params=_params(2),
        name="mix_ffn",
    )(oa, ob, x, lw["w_out"], lw["ln1_g"], lw["ln1_b"], lw["w_gate"], lw["w_up"], lw["w_down"],
      lw["ln2_g"], lw["ln2_b"])


def _t5_bucket(rel):
    nb = NUM_BUCKETS // 2
    max_exact = nb // 2
    bucket = (rel > 0).astype(jnp.int32) * nb
    n = jnp.abs(rel)
    n_f = jnp.maximum(n, 1).astype(F32)
    large = max_exact + (jnp.log(n_f / max_exact) / math.log(MAX_DISTANCE / max_exact)
                         * (nb - max_exact)).astype(jnp.int32)
    large = jnp.minimum(large, nb - 1)
    return bucket + jnp.where(n < max_exact, n, large)


def _window_tables(rel_table):
    key = jnp.arange(3 * WIN)[:, None]
    qry = jnp.arange(WIN)[None, :]
    rel = (key - WIN) - qry
    onehot = (_t5_bucket(rel)[..., None] == jnp.arange(NUM_BUCKETS)).astype(F32)
    bias = jnp.einsum("kqb,bh->kqh", onehot, rel_table.astype(F32),
                      precision=lax.Precision.HIGHEST)
    bias = bias.reshape(3, WIN, WIN, N_KV, GROUP).transpose(3, 0, 1, 4, 2)
    bias = bias.reshape(N_KV, 3, WIN, GROUP * WIN) * LOG2E
    mask = (jnp.abs(rel) <= WIN).astype(F32).reshape(3, WIN, 1, WIN)
    mask = jnp.broadcast_to(mask, (3, WIN, GROUP, WIN)).reshape(3, WIN, GROUP * WIN)
    return bias, mask


def _rope_tables(S):
    rows = S // GRID_W
    row = jnp.repeat(jnp.arange(rows), GRID_W).astype(F32)
    col = jnp.tile(jnp.arange(GRID_W), rows).astype(F32)
    axis_dim = HEAD_DIM // 2
    inv = ROPE_THETA ** (-jnp.arange(0, axis_dim, 2, dtype=F32) / axis_dim)
    ang = jnp.concatenate([row[:, None] * inv, col[:, None] * inv], axis=-1)
    return jnp.cos(ang).T, jnp.sin(ang).T


def _deinterleave_perm():
    within = np.concatenate([np.arange(0, HEAD_DIM, 2), np.arange(1, HEAD_DIM, 2)])
    return within


def _prep_w_in(w_in_l):
    within = _deinterleave_perm()
    cols = np.arange(w_in_l.shape[1])
    base_qb = Q_W + 2 * KV_W
    for h in range(N_Q + N_KV):
        lo = base_qb + h * HEAD_DIM
        cols[lo:lo + HEAD_DIM] = lo + within
    return w_in_l[:, cols].T.astype(BF16)


def _trunk(x, consts, layers, alpha):
    S = x.shape[1]
    cosT, sinT = _rope_tables(S)
    biasT, maskT = consts
    for lw in layers:
        qa, ka, va, qb, kb, vb = _inproj(x, lw["w_inT"], cosT, sinT, lw["qg"], lw["kg"])
        oa = _window_attn(qa, ka, va, biasT, maskT, lw["sink"])
        ob = _global_attn(qb, kb, vb)
        x = _mix_ffn(oa, ob, x, lw, alpha)
    return x


def kernel(x_prompt, x_sample, rel_bias_table, w_in, w_out, attn_sink, q_norm_g, k_norm_g,
           ln1_g, ln1_b, w_gate, w_up, w_down, ln2_g, ln2_b):
    depth = w_in.shape[0]
    alpha = (2.0 * depth) ** 0.25
    within = _deinterleave_perm()
    consts = _window_tables(rel_bias_table)
    layers = []
    for l in range(depth):
        sink = (attn_sink[l].astype(F32) * LOG2E).reshape(N_KV, GROUP, 1)
        sink = jnp.broadcast_to(sink, (N_KV, GROUP, WIN)).reshape(N_KV, 1, GROUP * WIN)
        layers.append(dict(
            w_inT=_prep_w_in(w_in[l]),
            qg=q_norm_g[l].astype(F32)[within].reshape(HEAD_DIM, 1),
            kg=k_norm_g[l].astype(F32)[within].reshape(HEAD_DIM, 1),
            sink=sink,
            w_out=w_out[l].astype(BF16),
            ln1_g=ln1_g[l].astype(F32).reshape(1, -1),
            ln1_b=ln1_b[l].astype(F32).reshape(1, -1),
            w_gate=w_gate[l].astype(BF16),
            w_up=w_up[l].astype(BF16),
            w_down=w_down[l].astype(BF16),
            ln2_g=ln2_g[l].astype(F32).reshape(1, -1),
            ln2_b=ln2_b[l].astype(F32).reshape(1, -1),
        ))
    y_prompt = _trunk(x_prompt, consts, layers, alpha)
    y_sample = _trunk(x_sample, consts, layers, alpha)
    return (y_prompt, y_sample)
```

```python
import functools
import math

import jax
import jax.numpy as jnp
import numpy as np
from jax import lax
from jax.experimental import pallas as pl
from jax.experimental.pallas import tpu as pltpu

F32 = jnp.float32
BF16 = jnp.bfloat16

HEAD_DIM = 64
HALF = HEAD_DIM // 2
N_Q = 8
N_KV = 2
GROUP = N_Q // N_KV
Q_W = N_Q * HEAD_DIM
KV_W = N_KV * HEAD_DIM
WIN = 128
NUM_BUCKETS = 32
MAX_DISTANCE = 128
GRID_W = 64
ROPE_THETA = 10000.0
LN_EPS = 1e-5
RMS_EPS = 1e-6
NEG_INF = -1e30
SCALE = HEAD_DIM ** -0.5
LOG2E = math.log2(math.e)
V7X_MXU_DIM = 256
V7X_VMEM_BYTES = 64 * 1024 * 1024
VMEM_LIMIT = V7X_VMEM_BYTES - 8 * 1024 * 1024
SPILL_HEADROOM = 6 * 1024 * 1024
V_ROWS = 80
CH = 512
KCH = V7X_MXU_DIM
ITEMS_PER_TRIP = 32
INPROJ_ROWS = 2048
FFN_ROWS = 2 * CH
WIN_ITEMS_PER_TRIP = 8


def _params(n_parallel):
    return pltpu.CompilerParams(
        dimension_semantics=("parallel",) * n_parallel,
        vmem_limit_bytes=VMEM_LIMIT)


def _inproj_kernel(x_ref, w_ref, cos_ref, sin_ref, qg_ref, kg_ref,
                   qa_ref, ka_ref, va_ref, qb_ref, kb_ref, vb_ref):
    tm = x_ref.shape[1]
    xb = x_ref[0].astype(BF16)

    def proj(row0, rows):
        return lax.dot_general(w_ref[row0:row0 + rows, :], xb, (((1,), (1,)), ((), ())),
                               preferred_element_type=F32)

    o_qa, o_ka = 0, Q_W
    o_qb = Q_W + 2 * KV_W
    o_kb = o_qb + Q_W

    def put_v(v_ref, vT):
        w = v_ref.shape[4]
        first_row = lax.broadcasted_iota(jnp.int32, (V_ROWS - HEAD_DIM, w), 0) == 0
        ones_pad = jnp.where(first_row, 1.0, 0.0).astype(BF16)
        for h in range(N_KV):
            for j in range(tm // w):
                v_ref[0, h, j, 0:HEAD_DIM, :] = (
                    vT[h * HEAD_DIM:(h + 1) * HEAD_DIM, j * w:(j + 1) * w].astype(BF16))
                v_ref[0, h, j, HEAD_DIM:V_ROWS, :] = ones_pad

    def put_k(k_ref, kT):
        kr = kT.T
        for h in range(N_KV):
            k_ref[0, h] = kr[:, h * HEAD_DIM:(h + 1) * HEAD_DIM].astype(BF16)

    c = cos_ref[...]
    s = sin_ref[...]

    def norm_rope(xT, g_ref, gain_scale):
        ms = jnp.mean(xT * xT, axis=0, keepdims=True)
        y = xT * lax.rsqrt(ms + RMS_EPS) * (g_ref[...] * gain_scale)
        y0 = y[:HALF]
        y1 = y[HALF:]
        return y0 * c - y1 * s, y0 * s + y1 * c

    heads_per_dot = GROUP
    for h0 in range(0, N_Q, heads_per_dot):
        qb = proj(o_qb + h0 * HEAD_DIM, heads_per_dot * HEAD_DIM)
        for hh in range(heads_per_dot):
            h = h0 + hh
            r0, r1 = norm_rope(qb[hh * HEAD_DIM:(hh + 1) * HEAD_DIM], qg_ref, SCALE * LOG2E)
            for jt in range(tm // CH):
                cols = slice(jt * CH, (jt + 1) * CH)
                qb_ref[0, jt, h * HEAD_DIM:h * HEAD_DIM + HALF, :] = r0[:, cols].astype(BF16)
                qb_ref[0, jt, h * HEAD_DIM + HALF:(h + 1) * HEAD_DIM, :] = r1[:, cols].astype(BF16)
    kvb = proj(o_kb, 2 * KV_W)
    k_rot = []
    for h in range(N_KV):
        r0, r1 = norm_rope(kvb[h * HEAD_DIM:(h + 1) * HEAD_DIM], kg_ref, 1.0)
        k_rot += [r0, r1]
    put_k(kb_ref, jnp.concatenate(k_rot, axis=0))
    put_v(vb_ref, kvb[KV_W:])

    kva = proj(o_ka, 2 * KV_W)
    put_k(ka_ref, kva[:KV_W])
    put_v(va_ref, kva[KV_W:])
    for h in range(N_KV):
        qa_s = (proj(o_qa + h * GROUP * HEAD_DIM, GROUP * HEAD_DIM)
                * (SCALE * LOG2E)).astype(BF16)
        for j in range(tm // WIN):
            for g in range(GROUP):
                qa_ref[0, j, h, :, g * WIN:(g + 1) * WIN] = (
                    qa_s[g * HEAD_DIM:(g + 1) * HEAD_DIM, j * WIN:(j + 1) * WIN])


def _inproj(x, w_inT, cosT, sinT, qg, kg):
    B, S, D = x.shape
    d_in = w_inT.shape[0]
    tm = INPROJ_ROWS
    grid = (B, S // tm)
    q_shape = jax.ShapeDtypeStruct((B, S // WIN, N_KV, HEAD_DIM, GROUP * WIN), BF16)
    k_shape = jax.ShapeDtypeStruct((B, N_KV, S, HEAD_DIM), BF16)
    va_shape = jax.ShapeDtypeStruct((B, N_KV, S // WIN, V_ROWS, WIN), BF16)
    vb_shape = jax.ShapeDtypeStruct((B, N_KV, S // KCH, V_ROWS, KCH), BF16)
    q_spec = pl.BlockSpec((1, tm // WIN, N_KV, HEAD_DIM, GROUP * WIN),
                          lambda b, i: (b, i, 0, 0, 0))
    qb_shape = jax.ShapeDtypeStruct((B, S // CH, Q_W, CH), BF16)
    qb_spec = pl.BlockSpec((1, tm // CH, Q_W, CH), lambda b, i: (b, i, 0, 0))
    k_spec = pl.BlockSpec((1, N_KV, tm, HEAD_DIM), lambda b, i: (b, 0, i, 0))
    va_spec = pl.BlockSpec((1, N_KV, tm // WIN, V_ROWS, WIN), lambda b, i: (b, 0, i, 0, 0))
    vb_spec = pl.BlockSpec((1, N_KV, tm // KCH, V_ROWS, KCH), lambda b, i: (b, 0, i, 0, 0))
    return pl.pallas_call(
        _inproj_kernel,
        grid=grid,
        in_specs=[
            pl.BlockSpec((1, tm, D), lambda b, i: (b, i, 0)),
            pl.BlockSpec((d_in, D), lambda b, i: (0, 0)),
            pl.BlockSpec((HALF, tm), lambda b, i: (0, i)),
            pl.BlockSpec((HALF, tm), lambda b, i: (0, i)),
            pl.BlockSpec((HEAD_DIM, 1), lambda b, i: (0, 0)),
            pl.BlockSpec((HEAD_DIM, 1), lambda b, i: (0, 0)),
        ],
        out_specs=[q_spec, k_spec, va_spec, qb_spec, k_spec, vb_spec],
        out_shape=[q_shape, k_shape, va_shape, qb_shape, k_shape, vb_shape],
        compiler_params=_params(2),
        name="inproj",
    )(x, w_inT, cosT, sinT, qg, kg)


def _global_attn_kernel(qT_ref, k_ref, vT_ref, o_ref, s_scr, p_scr, cmax_scr, alpha_scr, acc_scr):
    nq, _, tq = qT_ref.shape[1:]
    n = vT_ref.shape[2]
    per_tile = k_ref.shape[1] * GROUP * n
    n_items = nq * per_tile
    R = s_scr.shape[0]
    assert n % R == 0 or R % n == 0
    trips_per_head = max(n // R, 1)
    slots_per_head = min(n, R)

    def decode(item):
        rem = item % per_tile
        return item // per_tile, rem // n, rem % n

    def head_rows(g):
        return pl.ds(pl.multiple_of(g * HEAD_DIM, HEAD_DIM), HEAD_DIM)

    def scores(slot, item):
        qt, g, c = decode(item)
        kc = k_ref[0, g // GROUP, pl.ds(pl.multiple_of(c * KCH, KCH), KCH), :]
        s = jnp.dot(kc, qT_ref[0, qt, head_rows(g), :], preferred_element_type=F32)
        s_scr[slot] = s
        cmax_scr[slot] = jnp.max(s, axis=0, keepdims=True)

    def probs(slot, m_prev):
        m = jnp.maximum(m_prev, cmax_scr[slot])
        alpha_scr[slot] = jnp.exp2(m_prev - m)
        p_scr[slot] = jnp.exp2(s_scr[slot] - m).astype(BF16)
        return m

    def accumulate(slot, item, acc):
        _, g, c = decode(jnp.maximum(item, 0))
        return acc * alpha_scr[slot] + jnp.dot(vT_ref[0, g // GROUP, c], p_scr[slot],
                                               preferred_element_type=F32)

    def emit(slot, item, acc):
        if (slot + 1) % slots_per_head == 0:
            qt, g, _ = decode(jnp.maximum(item, 0))
            o_ref[0, qt, head_rows(g), :] = (
                acc[:HEAD_DIM] * (1.0 / acc[HEAD_DIM:HEAD_DIM + 1])).astype(BF16)

    def trip(u, m, with_scores):
        t0 = R * u
        acc = acc_scr[...]
        for j in range(R):
            if j % slots_per_head == 0:
                m = jnp.where(u % trips_per_head == 0, NEG_INF, m)
            acc = accumulate(j, t0 - R + j, acc)
            emit(j, t0 - R + j, acc)
            m = probs(j, m)
            if with_scores:
                scores(j, t0 + R + j)
        acc_scr[...] = acc
        return m

    n_trips = n_items // R
    for j in range(R):
        scores(j, j)
    p_scr[...] = jnp.zeros(p_scr.shape, BF16)
    alpha_scr[...] = jnp.ones(alpha_scr.shape, F32)
    acc_scr[...] = jnp.ones(acc_scr.shape, F32)
    m = lax.fori_loop(0, n_trips - 1, lambda u, m: trip(u, m, True),
                      jnp.full((1, tq), NEG_INF, F32))
    trip(n_trips - 1, m, False)
    acc = acc_scr[...]
    for j in range(R):
        acc = accumulate(j, n_items - R + j, acc)
        emit(j, n_items - R + j, acc)


def _global_attn(qT, k, vT):
    B, nq, _, tq = qT.shape
    S = k.shape[2]
    R = ITEMS_PER_TRIP
    lanes = 128
    per_kv_head = 2 * 2 * (2 * S * GROUP * HEAD_DIM + S * lanes + S * V_ROWS)
    scratch = R * KCH * tq * (4 + 2) + V_ROWS * tq * 4
    hk = N_KV if N_KV * per_kv_head + scratch <= VMEM_LIMIT - SPILL_HEADROOM else 1
    gw = hk * GROUP * HEAD_DIM
    return pl.pallas_call(
        _global_attn_kernel,
        grid=(B, N_KV // hk),
        in_specs=[
            pl.BlockSpec((1, nq, gw, tq), lambda b, h: (b, 0, h, 0)),
            pl.BlockSpec((1, hk, S, HEAD_DIM), lambda b, h: (b, h, 0, 0)),
            pl.BlockSpec((1, hk, S // KCH, V_ROWS, KCH), lambda b, h: (b, h, 0, 0, 0)),
        ],
        out_specs=pl.BlockSpec((1, nq, gw, tq), lambda b, h: (b, 0, h, 0)),
        out_shape=jax.ShapeDtypeStruct((B, nq, Q_W, tq), BF16),
        scratch_shapes=[pltpu.VMEM((R, KCH, tq), F32),
                        pltpu.VMEM((R, KCH, tq), BF16),
                        pltpu.VMEM((R, 1, tq), F32),
                        pltpu.VMEM((R, 1, tq), F32),
                        pltpu.VMEM((V_ROWS, tq), F32)],
        compiler_params=_params(2),
        name="global_attn",
    )(qT, k, vT)


def _window_attn_kernel(q_ref, k_ref, vT_ref, bias_ref, mask_ref, sink_ref, o_ref,
                        s_scr, p_scr, m_scr, sinkp_scr):
    nb = q_ref.shape[1]
    R = WIN_ITEMS_PER_TRIP
    assert nb % R == 0
    n_trips = nb // R
    sink = sink_ref[0]

    def key_block(i, c):
        kb = i - 1 + c
        return kb, jnp.clip(kb, 0, nb - 1)

    def scores(slot, i):
        q = q_ref[0, i, 0]
        m = sink
        for c in range(3):
            kb, kb_c = key_block(i, c)
            kc = k_ref[0, 0, pl.ds(pl.multiple_of(kb_c * WIN, WIN), WIN), :]
            s = jnp.dot(kc, q, preferred_element_type=F32) + bias_ref[0, c]
            if c != 1:
                thr = jnp.where(kb == kb_c, 0.5, 2.0)
                s = jnp.where(mask_ref[c] > thr, s, NEG_INF)
            s_scr[slot, c] = s
            m = jnp.maximum(m, jnp.max(s, axis=0, keepdims=True))
        m_scr[slot] = m

    def probs(slot):
        m = m_scr[slot]
        sinkp_scr[slot] = jnp.exp2(sink - m)
        for c in range(3):
            p_scr[slot, c] = jnp.exp2(s_scr[slot, c] - m).astype(BF16)

    def output(slot, i):
        i = jnp.maximum(i, 0)
        acc = None
        for c in range(3):
            _, kb_c = key_block(i, c)
            pv = jnp.dot(vT_ref[0, 0, kb_c], p_scr[slot, c], preferred_element_type=F32)
            acc = pv if acc is None else acc + pv
        denom = acc[HEAD_DIM:HEAD_DIM + 1] + sinkp_scr[slot]
        oT = (acc[:HEAD_DIM] * (1.0 / denom)).astype(BF16)
        for g in range(GROUP):
            o_ref[0, i, g * HEAD_DIM:(g + 1) * HEAD_DIM, :] = oT[:, g * WIN:(g + 1) * WIN]

    def trip(u, with_scores):
        t0 = R * u
        for j in range(R):
            output(j, t0 - R + j)
            probs(j)
            if with_scores:
                scores(j, t0 + R + j)

    for j in range(R):
        scores(j, j)
    p_scr[...] = jnp.zeros(p_scr.shape, BF16)
    sinkp_scr[...] = jnp.ones(sinkp_scr.shape, F32)

    def body(u, carry):
        trip(u, True)
        return carry

    lax.fori_loop(0, n_trips - 1, body, 0)
    trip(n_trips - 1, False)
    for j in range(R):
        output(j, nb - R + j)


def _window_attn(q, k, vT, biasT, maskT, sink_row):
    B, nb = q.shape[:2]
    S = k.shape[2]
    gw = GROUP * HEAD_DIM
    gl = GROUP * WIN
    R = WIN_ITEMS_PER_TRIP
    return pl.pallas_call(
        _window_attn_kernel,
        grid=(B, N_KV),
        in_specs=[
            pl.BlockSpec((1, nb, 1, HEAD_DIM, gl), lambda b, h: (b, 0, h, 0, 0)),
            pl.BlockSpec((1, 1, S, HEAD_DIM), lambda b, h: (b, h, 0, 0)),
            pl.BlockSpec((1, 1, nb, V_ROWS, WIN), lambda b, h: (b, h, 0, 0, 0)),
            pl.BlockSpec((1, 3, WIN, gl), lambda b, h: (h, 0, 0, 0)),
            pl.BlockSpec((3, WIN, gl), lambda b, h: (0, 0, 0)),
            pl.BlockSpec((1, 1, gl), lambda b, h: (h, 0, 0)),
        ],
        out_specs=pl.BlockSpec((1, nb, gw, WIN), lambda b, h: (b, 0, h, 0)),
        out_shape=jax.ShapeDtypeStruct((B, nb, Q_W, WIN), BF16),
        scratch_shapes=[pltpu.VMEM((R, 3, WIN, gl), F32),
                        pltpu.VMEM((R, 3, WIN, gl), BF16),
                        pltpu.VMEM((R, 1, gl), F32),
                        pltpu.VMEM((R, 1, gl), F32)],
        compiler_params=_params(2),
        name="window_attn",
    )(q, k, vT, biasT, maskT, sink_row)


def _layer_norm(v, g, b):
    mu = jnp.mean(v, axis=-1, keepdims=True)
    d = v - mu
    var = jnp.mean(d * d, axis=-1, keepdims=True)
    return d * lax.rsqrt(var + LN_EPS) * g + b


def _mix_ffn_kernel(oa_ref, ob_ref, x_ref, wo_ref, g1_ref, b1_ref, wg_ref, wu_ref, wd_ref,
                    g2_ref, b2_ref, y_ref, *, alpha, ff_chunk):
    d_ff = wg_ref.shape[1]
    wins_per_group = CH // WIN

    def mixed_norm(r):
        oaT = jnp.concatenate([oa_ref[0, r * wins_per_group + j] for j in range(wins_per_group)],
                              axis=1)
        oT = jnp.concatenate([oaT, ob_ref[0, r]], axis=0)
        mixed = lax.dot_general(oT, wo_ref[...], (((0,), (0,)), ((), ())),
                                preferred_element_type=F32)
        return _layer_norm(alpha * x_ref[0, r * CH:(r + 1) * CH, :] + mixed,
                           g1_ref[...], b1_ref[...])

    def zero_after(v):
        return jnp.sum(v, axis=0, keepdims=True) * 0.0

    def ffn(x1, first_after, last_after):
        xb = x1.astype(BF16)
        n_chunks = d_ff // ff_chunk
        acc = None
        for ci in range(n_chunks):
            c0 = ci * ff_chunk
            lhs = xb
            if ci == 0 and first_after is not None:
                lhs = (x1 + first_after).astype(BF16)
            if ci == n_chunks - 1 and last_after is not None:
                lhs = (x1 + last_after).astype(BF16)
            gate = jnp.dot(lhs, wg_ref[:, c0:c0 + ff_chunk], preferred_element_type=F32)
            up = jnp.dot(lhs, wu_ref[:, c0:c0 + ff_chunk], preferred_element_type=F32)
            h = (gate * jax.nn.sigmoid(gate) * up).astype(BF16)
            part = jnp.dot(h, wd_ref[c0:c0 + ff_chunk, :], preferred_element_type=F32)
            acc = part if acc is None else acc + part
        return acc

    x1a = mixed_norm(0)
    x1b = mixed_norm(1)
    acc_a = ffn(x1a, None, zero_after(x1b))
    y_ref[0, 0:CH, :] = _layer_norm(alpha * x1a + acc_a, g2_ref[...], b2_ref[...])
    acc_b = ffn(x1b, zero_after(acc_a), None)
    y_ref[0, CH:2 * CH, :] = _layer_norm(alpha * x1b + acc_b, g2_ref[...], b2_ref[...])


def _mix_ffn(oa, ob, x, lw, alpha):
    B, S, D = x.shape
    d_ff = lw["w_gate"].shape[1]
    tm = FFN_ROWS
    ff_chunk = V7X_MXU_DIM
    assert d_ff % ff_chunk == 0 and tm == 2 * CH
    row = lambda bb, i: (bb, i, 0)
    const = lambda bb, i: (0, 0)
    once = pl.Buffered(1)
    vec = pl.BlockSpec((1, D), const)
    return pl.pallas_call(
        functools.partial(_mix_ffn_kernel, alpha=alpha, ff_chunk=ff_chunk),
        grid=(B, S // tm),
        in_specs=[
            pl.BlockSpec((1, tm // WIN, Q_W, WIN), lambda bb, i: (bb, i, 0, 0)),
            pl.BlockSpec((1, tm // CH, Q_W, CH), lambda bb, i: (bb, i, 0, 0)),
            pl.BlockSpec((1, tm, D), row),
            pl.BlockSpec((2 * Q_W, D), const, pipeline_mode=once),
            vec, vec,
            pl.BlockSpec((D, d_ff), const, pipeline_mode=once),
            pl.BlockSpec((D, d_ff), const, pipeline_mode=once),
            pl.BlockSpec((d_ff, D), const, pipeline_mode=once),
            vec, vec,
        ],
        out_specs=pl.BlockSpec((1, tm, D), row),
        out_shape=jax.ShapeDtypeStruct((B, S, D), F32),
        compiler_params=_params(2),
        name="mix_ffn",
    )(oa, ob, x, lw["w_out"], lw["ln1_g"], lw["ln1_b"], lw["w_gate"], lw["w_up"], lw["w_down"],
      lw["ln2_g"], lw["ln2_b"])


def _t5_bucket(rel):
    nb = NUM_BUCKETS // 2
    max_exact = nb // 2
    bucket = (rel > 0).astype(jnp.int32) * nb
    n = jnp.abs(rel)
    n_f = jnp.maximum(n, 1).astype(F32)
    large = max_exact + (jnp.log(n_f / max_exact) / math.log(MAX_DISTANCE / max_exact)
                         * (nb - max_exact)).astype(jnp.int32)
    large = jnp.minimum(large, nb - 1)
    return bucket + jnp.where(n < max_exact, n, large)


def _window_tables(rel_table):
    key = jnp.arange(3 * WIN)[:, None]
    qry = jnp.arange(WIN)[None, :]
    rel = (key - WIN) - qry
    onehot = (_t5_bucket(rel)[..., None] == jnp.arange(NUM_BUCKETS)).astype(F32)
    bias = jnp.einsum("kqb,bh->kqh", onehot, rel_table.astype(F32),
                      precision=lax.Precision.HIGHEST)
    bias = bias.reshape(3, WIN, WIN, N_KV, GROUP).transpose(3, 0, 1, 4, 2)
    bias = bias.reshape(N_KV, 3, WIN, GROUP * WIN) * LOG2E
    mask = (jnp.abs(rel) <= WIN).astype(F32).reshape(3, WIN, 1, WIN)
    mask = jnp.broadcast_to(mask, (3, WIN, GROUP, WIN)).reshape(3, WIN, GROUP * WIN)
    return bias, mask


def _rope_tables(S):
    rows = S // GRID_W
    row = jnp.repeat(jnp.arange(rows), GRID_W).astype(F32)
    col = jnp.tile(jnp.arange(GRID_W), rows).astype(F32)
    axis_dim = HEAD_DIM // 2
    inv = ROPE_THETA ** (-jnp.arange(0, axis_dim, 2, dtype=F32) / axis_dim)
    ang = jnp.concatenate([row[:, None] * inv, col[:, None] * inv], axis=-1)
    return jnp.cos(ang).T, jnp.sin(ang).T


def _deinterleave_perm():
    within = np.concatenate([np.arange(0, HEAD_DIM, 2), np.arange(1, HEAD_DIM, 2)])
    return within


def _prep_w_in(w_in_l):
    within = _deinterleave_perm()
    cols = np.arange(w_in_l.shape[1])
    base_qb = Q_W + 2 * KV_W
    for h in range(N_Q + N_KV):
        lo = base_qb + h * HEAD_DIM
        cols[lo:lo + HEAD_DIM] = lo + within
    return w_in_l[:, cols].T.astype(BF16)


def _trunk(x, consts, layers, alpha):
    S = x.shape[1]
    cosT, sinT = _rope_tables(S)
    biasT, maskT = consts
    for lw in layers:
        qa, ka, va, qb, kb, vb = _inproj(x, lw["w_inT"], cosT, sinT, lw["qg"], lw["kg"])
        oa = _window_attn(qa, ka, va, biasT, maskT, lw["sink"])
        ob = _global_attn(qb, kb, vb)
        x = _mix_ffn(oa, ob, x, lw, alpha)
    return x


def kernel(x_prompt, x_sample, rel_bias_table, w_in, w_out, attn_sink, q_norm_g, k_norm_g,
           ln1_g, ln1_b, w_gate, w_up, w_down, ln2_g, ln2_b):
    depth = w_in.shape[0]
    alpha = (2.0 * depth) ** 0.25
    within = _deinterleave_perm()
    consts = _window_tables(rel_bias_table)
    layers = []
    for l in range(depth):
        sink = (attn_sink[l].astype(F32) * LOG2E).reshape(N_KV, GROUP, 1)
        sink = jnp.broadcast_to(sink, (N_KV, GROUP, WIN)).reshape(N_KV, 1, GROUP * WIN)
        layers.append(dict(
            w_inT=_prep_w_in(w_in[l]),
            qg=q_norm_g[l].astype(F32)[within].reshape(HEAD_DIM, 1),
            kg=k_norm_g[l].astype(F32)[within].reshape(HEAD_DIM, 1),
            sink=sink,
            w_out=w_out[l].astype(BF16),
            ln1_g=ln1_g[l].astype(F32).reshape(1, -1),
            ln1_b=ln1_b[l].astype(F32).reshape(1, -1),
            w_gate=w_gate[l].astype(BF16),
            w_up=w_up[l].astype(BF16),
            w_down=w_down[l].astype(BF16),
            ln2_g=ln2_g[l].astype(F32).reshape(1, -1),
            ln2_b=ln2_b[l].astype(F32).reshape(1, -1),
        ))
    y_prompt = _trunk(x_prompt, consts, layers, alpha)
    y_sample = _trunk(x_sample, consts, layers, alpha)
    return (y_prompt, y_sample)
```
